```python
import jax, jax.numpy as jnp
from jax import lax
import numpy as np

D_MODEL = 1024
BATCH = 1
SEQ = 16384
DEPTH = 1
DEC_BATCH = 128
DEC_SEQ = 4
PAST_LEN = 16384
PAGE_SIZE = 128

H_A = 8
DH_A = 64
ROT_A = DH_A // 4
THETA_A = 500000.0
MOBA_BLOCK = 256
MOBA_TOPK = 3
H_B = 8
Q_RANK = 256
KV_RANK = 128
NOPE_B = 64
ROPE_B = 32
V_B = 64
THETA_B = 10000.0
D_FF = 4 * D_MODEL
Q_BLOCK = 128
EPS = 1e-6
NEG_INF = -1e30
IN_SIZES = (H_A * DH_A, H_A * DH_A, H_A * DH_A, Q_RANK, KV_RANK, ROPE_B, D_MODEL, D_MODEL)
D_IN = 3 * H_A * DH_A + Q_RANK + KV_RANK + ROPE_B + 2 * D_MODEL

kernel_name = 'hybrid_moba_mla_gated_decoder_step'


def rms_norm(x, g):
    xf = x.astype(jnp.float32)
    y = xf * lax.rsqrt(jnp.mean(xf * xf, axis=-1, keepdims=True) + EPS)
    return (y * g.astype(jnp.float32)).astype(x.dtype)


def rope(x, pos, theta):
    half = x.shape[-1] // 2
    inv = 1.0 / (theta ** (jnp.arange(half, dtype=jnp.float32) / half))
    ang = pos.astype(jnp.float32)[:, None] * inv[None, :]
    cos = jnp.cos(ang)[:, None, :]
    sin = jnp.sin(ang)[:, None, :]
    xf = x.astype(jnp.float32)
    x1, x2 = xf[..., :half], xf[..., half:]
    return jnp.concatenate([x1 * cos - x2 * sin, x2 * cos + x1 * sin], axis=-1).astype(x.dtype)


def partial_rope(x, pos):
    return jnp.concatenate([rope(x[..., :ROT_A], pos, THETA_A), x[..., ROT_A:]], axis=-1)


def adaln_modulation(c, w_ada, b_ada):
    m = jnp.einsum('bd,de->be', jax.nn.silu(c), w_ada) + b_ada
    sh1, sc1, gt1, sh2, sc2, gt2 = jnp.split(m[:, None, :], 6, axis=-1)
    return sh1, sc1, gt1, sh2, sc2, gt2


def mixer_project(h, pos, w_in, g_q_lat, g_kv_lat, w_uq, w_uk):
    B, S, _ = h.shape
    p = jnp.einsum('bsd,de->bse', h, w_in)
    offsets = [int(o) for o in np.cumsum(IN_SIZES)[:-1]]
    qa, ka, va, q_lat, kv_lat, k_rope, gate_a, gate_b = jnp.split(p, offsets, axis=-1)
    qa = partial_rope(qa.reshape(B, S, H_A, DH_A), pos)
    ka = partial_rope(ka.reshape(B, S, H_A, DH_A), pos)
    va = va.reshape(B, S, H_A, DH_A)
    qb = jnp.einsum('bsr,re->bse', rms_norm(q_lat, g_q_lat), w_uq).reshape(B, S, H_B, NOPE_B + ROPE_B)
    q_rope = rope(qb[..., NOPE_B:], pos, THETA_B)
    q_abs = jnp.einsum('bshn,rhn->bshr', qb[..., :NOPE_B], w_uk)
    latent = rms_norm(kv_lat, g_kv_lat)
    k_rope = rope(k_rope[:, :, None, :], pos, THETA_B)[:, :, 0, :]
    return qa, ka, va, q_abs, q_rope, latent, k_rope, gate_a, gate_b


def mixer_merge(o_a, o_b_lat, gate_a, gate_b, w_uv, w_proj_a, w_proj_b, w_out):
    B, S = o_a.shape[:2]
    y_a = jnp.einsum('bse,ed->bsd', o_a.reshape(B, S, H_A * DH_A), w_proj_a)
    o_b = jnp.einsum('bshr,rhv->bshv', o_b_lat, w_uv).reshape(B, S, H_B * V_B)
    y_b = jnp.einsum('bse,ed->bsd', o_b, w_proj_b)
    merged = jax.nn.sigmoid(gate_a) * y_a + jax.nn.sigmoid(gate_b) * y_b
    return jnp.einsum('bsd,de->bse', merged, w_out)


def sqrelu_mlp(h, w_up, w_down):
    u = jax.nn.relu(jnp.einsum('bsd,df->bsf', h, w_up))
    return jnp.einsum('bsf,fd->bsd', u * u, w_down)


def moba_combine(q, q_pos, k_own, v_own, own_pos, k_sel, v_sel, sel_valid):
    scale = DH_A ** -0.5
    lg_own = jnp.einsum('bqhd,bkhd->bhqk', q, k_own).astype(jnp.float32) * scale
    lg_own = jnp.where(own_pos[None, :] <= q_pos[:, None], lg_own, NEG_INF)
    if k_sel is None:
        p = jax.nn.softmax(lg_own, axis=-1).astype(v_own.dtype)
        return jnp.einsum('bhqk,bkhd->bqhd', p, v_own)
    B, Q, H, N, L, _ = k_sel.shape
    lg_sel = jnp.einsum('bqhd,bqhnkd->bhqnk', q, k_sel).astype(jnp.float32) * scale
    if sel_valid is not None:
        lg_sel = jnp.where(jnp.transpose(sel_valid, (0, 2, 1, 3))[..., None], lg_sel, NEG_INF)
    lg = jnp.concatenate([lg_sel.reshape(B, H, Q, N * L), lg_own], axis=-1)
    p = jax.nn.softmax(lg, axis=-1).astype(v_own.dtype)
    p_sel = p[..., :N * L].reshape(B, H, Q, N, L)
    p_own = p[..., N * L:]
    return (jnp.einsum('bhqnk,bqhnkd->bqhd', p_sel, v_sel)
            + jnp.einsum('bhqk,bkhd->bqhd', p_own, v_own))


def moba_prompt(q, k, v):
    B, S, H, Dh = q.shape
    nb = -(-S // MOBA_BLOCK)
    pad = nb * MOBA_BLOCK - S
    k_blk = jnp.pad(k, ((0, 0), (0, pad), (0, 0), (0, 0))).reshape(B, nb, MOBA_BLOCK, H, Dh)
    v_blk = jnp.pad(v, ((0, 0), (0, pad), (0, 0), (0, 0))).reshape(B, nb, MOBA_BLOCK, H, Dh)
    k_mean = jnp.mean(k_blk.astype(jnp.float32), axis=2)
    topk = min(MOBA_TOPK, nb - 1)
    b_idx = jnp.arange(B)[:, None, None, None]
    h_idx = jnp.arange(H)[None, None, :, None]
    blk_ids = jnp.arange(nb)

    def one_block(qb):
        q0 = qb * Q_BLOCK
        qc = lax.dynamic_slice_in_dim(q, q0, Q_BLOCK, axis=1)
        q_pos = q0 + jnp.arange(Q_BLOCK)
        own = q0 // MOBA_BLOCK
        k_own = lax.dynamic_index_in_dim(k_blk, own, axis=1, keepdims=False)
        v_own = lax.dynamic_index_in_dim(v_blk, own, axis=1, keepdims=False)
        own_pos = own * MOBA_BLOCK + jnp.arange(MOBA_BLOCK)
        if topk == 0:
            return moba_combine(qc, q_pos, k_own, v_own, own_pos, None, None, None)
        s = jnp.einsum('bqhd,bnhd->bqhn', qc.astype(jnp.float32), k_mean)
        s = jnp.where(blk_ids < own, s, NEG_INF)
        _, sel = lax.top_k(s, topk)
        valid = sel < own
        k_sel = k_blk[b_idx, sel, :, h_idx]
        v_sel = v_blk[b_idx, sel, :, h_idx]
        return moba_combine(qc, q_pos, k_own, v_own, own_pos, k_sel, v_sel, valid)

    out = lax.map(one_block, jnp.arange(S // Q_BLOCK))
    return jnp.moveaxis(out, 0, 1).reshape(B, S, H, Dh)


def moba_sample(q, k_new, v_new, cache_k, cache_v, page_table):
    DB, T, H, Dh = q.shape
    n_pages = page_table.shape[1]
    past = n_pages * PAGE_SIZE
    ppb = MOBA_BLOCK // PAGE_SIZE
    own = past // MOBA_BLOCK
    own_past_pages = (past - own * MOBA_BLOCK) // PAGE_SIZE
    q_pos = past + jnp.arange(T)
    if own_past_pages > 0:
        pages = page_table[:, n_pages - own_past_pages:]
        k_op = cache_k[pages].reshape(DB, own_past_pages * PAGE_SIZE, H, Dh)
        v_op = cache_v[pages].reshape(DB, own_past_pages * PAGE_SIZE, H, Dh)
        k_own = jnp.concatenate([k_op, k_new], axis=1)
        v_own = jnp.concatenate([v_op, v_new], axis=1)
        own_pos = jnp.concatenate([own * MOBA_BLOCK + jnp.arange(own_past_pages * PAGE_SIZE), q_pos])
    else:
        k_own, v_own, own_pos = k_new, v_new, q_pos
    topk = min(MOBA_TOPK, own)
    if topk == 0:
        return moba_combine(q, q_pos, k_own, v_own, own_pos, None, None, None)
    page_sum = jnp.sum(cache_k, axis=1, dtype=jnp.float32)
    k_mean = page_sum[page_table[:, :own * ppb]].reshape(DB, own, ppb, H, Dh).sum(axis=2) / MOBA_BLOCK
    s = jnp.einsum('bqhd,bnhd->bqhn', q.astype(jnp.float32), k_mean)
    _, sel = lax.top_k(s, topk)
    b_idx = jnp.arange(DB)[:, None, None, None, None]
    phys = page_table[b_idx, sel[..., None] * ppb + jnp.arange(ppb)]
    h_idx = jnp.arange(H)[None, None, :, None, None]
    k_sel = cache_k[phys, :, h_idx].reshape(DB, T, H, topk, MOBA_BLOCK, Dh)
    v_sel = cache_v[phys, :, h_idx].reshape(DB, T, H, topk, MOBA_BLOCK, Dh)
    return moba_combine(q, q_pos, k_own, v_own, own_pos, k_sel, v_sel, None)


def mla_logits(q_abs, q_rope, latent, k_rope):
    scale = (NOPE_B + ROPE_B) ** -0.5
    lg = jnp.einsum('bqhr,bkr->bhqk', q_abs, latent) + jnp.einsum('bqhe,bke->bhqk', q_rope, k_rope)
    return lg.astype(jnp.float32) * scale


def mla_prompt(q_abs, q_rope, latent, k_rope):
    B, S = q_abs.shape[:2]
    k_pos = jnp.arange(S)

    def one_block(qb):
        q0 = qb * Q_BLOCK
        ql = lax.dynamic_slice_in_dim(q_abs, q0, Q_BLOCK, axis=1)
        qr = lax.dynamic_slice_in_dim(q_rope, q0, Q_BLOCK, axis=1)
        q_pos = q0 + jnp.arange(Q_BLOCK)
        lg = mla_logits(ql, qr, latent, k_rope)
        lg = jnp.where(k_pos[None, :] <= q_pos[:, None], lg, NEG_INF)
        p = jax.nn.softmax(lg, axis=-1).astype(latent.dtype)
        return jnp.einsum('bhqk,bkr->bqhr', p, latent)

    out = lax.map(one_block, jnp.arange(S // Q_BLOCK))
    return jnp.moveaxis(out, 0, 1).reshape(B, S, H_B, KV_RANK)


def mla_sample(q_abs, q_rope, lat_new, kr_new, cache_latent, cache_krope, page_table):
    DB, T = q_abs.shape[:2]
    past = page_table.shape[1] * PAGE_SIZE
    lat_past = cache_latent[page_table].reshape(DB, past, KV_RANK)
    kr_past = cache_krope[page_table].reshape(DB, past, ROPE_B)
    lg_past = mla_logits(q_abs, q_rope, lat_past, kr_past)
    lg_new = mla_logits(q_abs, q_rope, lat_new, kr_new)
    causal = jnp.arange(T)[None, :] <= jnp.arange(T)[:, None]
    lg_new = jnp.where(causal, lg_new, NEG_INF)
    p = jax.nn.softmax(jnp.concatenate([lg_past, lg_new], axis=-1), axis=-1).astype(lat_new.dtype)
    return (jnp.einsum('bhqk,bkr->bqhr', p[..., :past], lat_past)
            + jnp.einsum('bhqk,bkr->bqhr', p[..., past:], lat_new))


def trunk_layer(x, c, pos, attend, w_ada, b_ada, g_pre_mix, g_post_mix, g_pre_mlp, g_post_mlp,
                w_in, g_q_lat, g_kv_lat, w_uq, w_uk, w_uv, w_proj_a, w_proj_b, w_out, w_up, w_down):
    sh1, sc1, gt1, sh2, sc2, gt2 = adaln_modulation(c, w_ada, b_ada)
    h = rms_norm(x, g_pre_mix) * (1.0 + sc1) + sh1
    qa, ka, va, q_abs, q_rope, latent, k_rope, gate_a, gate_b = mixer_project(
        h, pos, w_in, g_q_lat, g_kv_lat, w_uq, w_uk)
    o_a, o_b_lat = attend(qa, ka, va, q_abs, q_rope, latent, k_rope)
    m = mixer_merge(o_a, o_b_lat, gate_a, gate_b, w_uv, w_proj_a, w_proj_b, w_out)
    x = x + gt1 * rms_norm(m, g_post_mix)
    h2 = rms_norm(x, g_pre_mlp) * (1.0 + sc2) + sh2
    x = x + gt2 * rms_norm(sqrelu_mlp(h2, w_up, w_down), g_post_mlp)
    return x, ka, va, latent, k_rope


def setup_inputs(seed: int = 0) -> dict:
    key = jax.random.key(seed)
    ks = jax.random.split(key, 32)
    f32 = jnp.float32
    n_pages = PAST_LEN // PAGE_SIZE
    used = DEC_BATCH * n_pages
    n_phys = used + max(1, used // 4)
    nrm = lambda k, shape, s=1.0: jax.random.normal(k, shape, f32) * s
    gain = lambda k, n: 1.0 + 0.05 * jax.random.normal(k, (n,), f32)
    page_table = jax.random.permutation(ks[9], n_phys)[:used].reshape(DEC_BATCH, n_pages).astype(jnp.int32)
    return {
        'x_prompt': nrm(ks[0], (BATCH, SEQ, D_MODEL)),
        'x_sample': nrm(ks[1], (DEC_BATCH, DEC_SEQ, D_MODEL)),
        'c_prompt': nrm(ks[2], (BATCH, D_MODEL)),
        'c_sample': nrm(ks[3], (DEC_BATCH, D_MODEL)),
        'cache_k': nrm(ks[4], (n_phys, PAGE_SIZE, H_A, DH_A)),
        'cache_v': nrm(ks[5], (n_phys, PAGE_SIZE, H_A, DH_A)),
        'cache_latent': nrm(ks[6], (n_phys, PAGE_SIZE, KV_RANK)),
        'cache_krope': nrm(ks[7], (n_phys, PAGE_SIZE, ROPE_B)),
        'page_table': page_table,
        'w_ada': nrm(ks[10], (D_MODEL, 6 * D_MODEL), 0.5 * D_MODEL ** -0.5),
        'b_ada': nrm(ks[11], (6 * D_MODEL,), 0.01),
        'g_pre_mix': gain(ks[12], D_MODEL),
        'g_post_mix': gain(ks[13], D_MODEL),
        'g_pre_mlp': gain(ks[14], D_MODEL),
        'g_post_mlp': gain(ks[15], D_MODEL),
        'w_in': nrm(ks[16], (D_MODEL, D_IN), D_MODEL ** -0.5),
        'g_q_lat': gain(ks[17], Q_RANK),
        'g_kv_lat': gain(ks[18], KV_RANK),
        'w_uq': nrm(ks[19], (Q_RANK, H_B * (NOPE_B + ROPE_B)), Q_RANK ** -0.5),
        'w_uk': nrm(ks[20], (KV_RANK, H_B, NOPE_B), KV_RANK ** -0.5),
        'w_uv': nrm(ks[21], (KV_RANK, H_B, V_B), KV_RANK ** -0.5),
        'w_proj_a': nrm(ks[22], (H_A * DH_A, D_MODEL), (H_A * DH_A) ** -0.5),
        'w_proj_b': nrm(ks[23], (H_B * V_B, D_MODEL), (H_B * V_B) ** -0.5),
        'w_out': nrm(ks[24], (D_MODEL, D_MODEL), D_MODEL ** -0.5),
        'w_up': nrm(ks[25], (D_MODEL, D_FF), D_MODEL ** -0.5),
        'w_down': nrm(ks[26], (D_FF, D_MODEL), D_FF ** -0.5),
    }


def reference(x_prompt, x_sample, c_prompt, c_sample, cache_k, cache_v, cache_latent, cache_krope,
              page_table, w_ada, b_ada, g_pre_mix, g_post_mix, g_pre_mlp, g_post_mlp, w_in, g_q_lat,
              g_kv_lat, w_uq, w_uk, w_uv, w_proj_a, w_proj_b, w_out, w_up, w_down):
    past_len = page_table.shape[1] * PAGE_SIZE
    pos_prompt = jnp.arange(x_prompt.shape[1], dtype=jnp.int32)
    pos_sample = past_len + jnp.arange(x_sample.shape[1], dtype=jnp.int32)

    def attend_prompt(qa, ka, va, q_abs, q_rope, latent, k_rope):
        return moba_prompt(qa, ka, va), mla_prompt(q_abs, q_rope, latent, k_rope)

    def attend_sample(qa, ka, va, q_abs, q_rope, latent, k_rope):
        return (moba_sample(qa, ka, va, cache_k, cache_v, page_table),
                mla_sample(q_abs, q_rope, latent, k_rope, cache_latent, cache_krope, page_table))

    y_prompt, k_prompt, v_prompt, latent_prompt, krope_prompt = trunk_layer(
        x_prompt, c_prompt, pos_prompt, attend_prompt, w_ada, b_ada, g_pre_mix, g_post_mix,
        g_pre_mlp, g_post_mlp, w_in, g_q_lat, g_kv_lat, w_uq, w_uk, w_uv, w_proj_a, w_proj_b,
        w_out, w_up, w_down)
    y_sample, k_sample, v_sample, latent_sample, krope_sample = trunk_layer(
        x_sample, c_sample, pos_sample, attend_sample, w_ada, b_ada, g_pre_mix, g_post_mix,
        g_pre_mlp, g_post_mlp, w_in, g_q_lat, g_kv_lat, w_uq, w_uk, w_uv, w_proj_a, w_proj_b,
        w_out, w_up, w_down)
    return (y_prompt, y_sample, k_prompt, v_prompt, latent_prompt, krope_prompt,
            k_sample, v_sample, latent_sample, krope_sample)
```

```python
import functools

import jax
import jax.numpy as jnp
from jax import lax
from jax.experimental import pallas as pl
from jax.experimental.pallas import tpu as pltpu

F32 = jnp.float32
BF16 = jnp.bfloat16

D_MODEL = 1024
H_A = 8
DH_A = 64
ROT_A = DH_A // 4
THETA_A = 500000.0
MOBA_BLOCK = 256
MOBA_TOPK = 3
H_B = 8
Q_RANK = 256
KV_RANK = 128
NOPE_B = 64
ROPE_B = 32
V_B = 64
THETA_B = 10000.0
D_FF = 4 * D_MODEL
PAGE_SIZE = 128
EPS = 1e-6
NEG_INF = -1e30

LANES = 128
ROW_TILE = 256
N_SEL = 64
VMEM_LIMIT = 56 * 1024 * 1024

C_QA, C_KA, C_VA = 0, 512, 1024
C_QLAT, C_KVLAT = 1536, 1792
C_GA, C_GB, C_KR = 1920, 2944, 3968
D_IN_PAD = 4096
ONES_LANE_A = DH_A
ONES_LANE_B = ROPE_B


def _cparams(sem, vmem=VMEM_LIMIT):
    return pltpu.CompilerParams(dimension_semantics=sem, vmem_limit_bytes=vmem)


def _rms(x, g):
    return x * lax.rsqrt(jnp.mean(x * x, axis=-1, keepdims=True) + EPS) * g


def _dot(a, b):
    return jnp.dot(a, b, preferred_element_type=F32)


def _dot_nt(a, b, precision=None):
    return lax.dot_general(a, b, (((1,), (1,)), ((), ())), precision=precision,
                           preferred_element_type=F32)


def _lane_iota(shape):
    return lax.broadcasted_iota(jnp.int32, shape, len(shape) - 1)


def _rope_slab(x, c, s_up, s_dn, half):
    return x * c + pltpu.roll(x, half, 1) * s_up + pltpu.roll(x, LANES - half, 1) * s_dn


def _rope_tables(pos, rot, theta, group):
    half = rot // 2
    inv = 1.0 / (theta ** (jnp.arange(half, dtype=F32) / half))
    ang = pos.astype(F32)[:, None] * inv[None, :]
    cos, sin = jnp.cos(ang), jnp.sin(ang)
    n = pos.shape[0]
    z_half = jnp.zeros((n, half), F32)
    rest1 = jnp.ones((n, group - rot), F32)
    rest0 = jnp.zeros((n, group - rot), F32)
    c = jnp.concatenate([cos, cos, rest1], axis=1)
    s_up = jnp.concatenate([z_half, sin, rest0], axis=1)
    s_dn = jnp.concatenate([-sin, z_half, rest0], axis=1)
    rep = LANES // group
    return tuple(jnp.tile(t, (1, rep)) for t in (c, s_up, s_dn))


def _mod_kernel(c_ref, w_ref, b_ref, o_ref):
    c = c_ref[...]
    a = (c * jax.nn.sigmoid(c)).astype(BF16)
    o_ref[...] = _dot(a, w_ref[...].astype(BF16)) + b_ref[...]


def _modulation(c_all, w_ada, b_ada):
    rows = c_all.shape[0]
    n = w_ada.shape[1]
    tn = 768
    return pl.pallas_call(
        _mod_kernel,
        grid=(n // tn,),
        in_specs=[pl.BlockSpec((rows, D_MODEL), lambda j: (0, 0)),
                  pl.BlockSpec((D_MODEL, tn), lambda j: (0, j)),
                  pl.BlockSpec((1, tn), lambda j: (0, j))],
        out_specs=pl.BlockSpec((rows, tn), lambda j: (0, j)),
        out_shape=jax.ShapeDtypeStruct((rows, n), F32),
        compiler_params=_cparams(("arbitrary",)),
        name="adaln_modulation",
    )(c_all, w_ada, b_ada.reshape(1, n))


def _proj_in_kernel(prompt, x_ref, sc_ref, sh_ref, g_ref, w_ref, gq_ref, gkv_ref, wuqn_ref, wuqr_ref,
                    wuk_ref, ca_ref, sau_ref, sad_ref, cb_ref, sbu_ref, sbd_ref,
                    qa_ref, k_ref, v_ref, lat_ref, kr_ref, ga_ref, gb_ref, qcat_ref, kcat_ref,
                    *moba_refs):
    tm = x_ref.shape[0]
    x = x_ref[...]
    h = _rms(x, g_ref[...]) * (1.0 + sc_ref[...]) + sh_ref[...]
    p = _dot(h.astype(BF16), w_ref[...])

    ca, sau, sad = ca_ref[...], sau_ref[...], sad_ref[...]
    cb, sbu, sbd = cb_ref[...], sbu_ref[...], sbd_ref[...]

    def rope_a(y):
        return jnp.concatenate(
            [_rope_slab(y[:, s * LANES:(s + 1) * LANES], ca, sau, sad, ROT_A // 2)
             for s in range(H_A * DH_A // LANES)], axis=1)

    qa = rope_a(p[:, C_QA:C_QA + 512])
    ka = rope_a(p[:, C_KA:C_KA + 512])
    va = p[:, C_VA:C_VA + 512]
    qa_ref[...] = qa
    k_ref[...] = ka
    v_ref[...] = va
    ga_ref[...] = p[:, C_GA:C_GA + D_MODEL]
    gb_ref[...] = p[:, C_GB:C_GB + D_MODEL]

    latent = _rms(p[:, C_KVLAT:C_KVLAT + KV_RANK], gkv_ref[...])
    lat_ref[...] = latent
    kr = _rope_slab(p[:, C_KR:C_KR + LANES], cb, sbu, sbd, ROPE_B // 2)
    kr_ref[...] = kr[:, :ROPE_B]
    lane = _lane_iota((tm, LANES))
    kr_slab = jnp.where(lane == ONES_LANE_B, 1.0, kr)
    kcat_ref[...] = jnp.concatenate([latent, kr_slab], axis=1).astype(BF16)

    scale_b = (NOPE_B + ROPE_B) ** -0.5
    qn = _rms(p[:, C_QLAT:C_QLAT + Q_RANK], gq_ref[...]).astype(BF16)
    q_nope = _dot(qn, wuqn_ref[...])
    q_rope = _dot(qn, wuqr_ref[...])
    q_abs = _dot(q_nope.astype(BF16), wuk_ref[...])
    for hh in range(H_B):
        sl = slice(hh * LANES, (hh + 1) * LANES)
        qr = _rope_slab(q_rope[:, sl], cb, sbu, sbd, ROPE_B // 2)
        qcat_ref[hh] = (jnp.concatenate([q_abs[:, sl], qr], axis=1) * scale_b).astype(BF16)

    if prompt:
        kaug_ref, vaug_ref, kmean_ref = moba_refs
        blk = pl.program_id(0)
        onehot = jnp.where(lane == N_SEL + blk, 1.0, 0.0)
        ones_col = jnp.where(lane == ONES_LANE_A, 1.0, 0.0)
        for hh in range(H_A):
            sl = slice((hh // 2) * LANES, (hh // 2 + 1) * LANES)
            ks, vs = ka[:, sl], va[:, sl]
            if hh % 2 == 1:
                ks = pltpu.roll(ks, DH_A, 1)
                vs = pltpu.roll(vs, DH_A, 1)
            kaug_ref[hh] = jnp.where(lane < DH_A, ks, onehot).astype(BF16)
            vaug_ref[hh] = jnp.where(lane < DH_A, vs, ones_col).astype(BF16)
        kmean_ref[0] = jnp.sum(ka, axis=0, keepdims=True) * (1.0 / MOBA_BLOCK)


def _proj_in(x2, sc, sh, g_pre, w_in_p, g_q, g_kv, wuq_n, wuq_r, wuk_bd, tabs_a, tabs_b, prompt):
    rows = x2.shape[0]
    tm = ROW_TILE
    nt = rows // tm
    row = lambda w: pl.BlockSpec((tm, w), lambda i: (i, 0))
    const = lambda a: pl.BlockSpec(a.shape, lambda i: (0,) * a.ndim)
    mod = (lambda a: const(a)) if sc.shape[0] == 1 else (lambda a: row(a.shape[1]))
    heads = lambda w: pl.BlockSpec((H_A, tm, w), lambda i: (0, i, 0))
    in_specs = [row(D_MODEL), mod(sc), mod(sh), const(g_pre), const(w_in_p), const(g_q), const(g_kv),
                const(wuq_n), const(wuq_r), const(wuk_bd)] + [row(LANES)] * 6
    out_shape = [jax.ShapeDtypeStruct((rows, 512), F32)] * 3 + [
        jax.ShapeDtypeStruct((rows, KV_RANK), F32), jax.ShapeDtypeStruct((rows, ROPE_B), F32),
        jax.ShapeDtypeStruct((rows, D_MODEL), F32), jax.ShapeDtypeStruct((rows, D_MODEL), F32),
        jax.ShapeDtypeStruct((H_B, rows, 2 * LANES), BF16), jax.ShapeDtypeStruct((rows, 2 * LANES), BF16)]
    out_specs = [row(512)] * 3 + [row(KV_RANK), row(ROPE_B), row(D_MODEL), row(D_MODEL),
                                  heads(2 * LANES), row(2 * LANES)]
    if prompt:
        out_shape += [jax.ShapeDtypeStruct((H_A, rows, LANES), BF16)] * 2 + [
            jax.ShapeDtypeStruct((nt, 1, 512), F32)]
        out_specs += [heads(LANES), heads(LANES), pl.BlockSpec((1, 1, 512), lambda i: (i, 0, 0))]
    return pl.pallas_call(
        functools.partial(_proj_in_kernel, prompt),
        grid=(nt,), in_specs=in_specs, out_specs=out_specs, out_shape=out_shape,
        compiler_params=_cparams(("arbitrary",)),
        name="proj_in_prompt" if prompt else "proj_in_sample",
    )(x2, sc, sh, g_pre, w_in_p, g_q, g_kv, wuq_n, wuq_r, wuk_bd, *tabs_a, *tabs_b)


def _top3_lanes(scores, valid):
    lane = _lane_iota(scores.shape)
    s = jnp.where(valid, scores, NEG_INF)
    picked = jnp.zeros(scores.shape, jnp.bool_)
    idxs = []
    for _ in range(MOBA_TOPK):
        mx = jnp.max(s, axis=1, keepdims=True)
        idx = jnp.min(jnp.where(s == mx, lane, 2 * LANES), axis=1, keepdims=True)
        hit = lane == idx
        picked = jnp.logical_or(picked, hit)
        s = jnp.where(hit, -jnp.inf, s)
        idxs.append(idx)
    return picked, idxs


def _select_prompt_kernel(qa_ref, kmean_ref, qaug_ref):
    tm = qa_ref.shape[0]
    own = pl.program_id(0)
    lane = _lane_iota((tm, LANES))
    blk = lane - N_SEL
    scale = DH_A ** -0.5
    for hh in range(H_A):
        sl = slice((hh // 2) * LANES, (hh // 2 + 1) * LANES)
        qs = qa_ref[:, sl]
        km = kmean_ref[:, sl]
        lo = (hh % 2) == 0
        half_mask = (lane < DH_A) if lo else (lane >= DH_A)
        scores = _dot_nt(jnp.where(half_mask, qs, 0.0), km, precision=lax.Precision.HIGHEST)
        valid = jnp.logical_and(blk >= 0, blk < own)
        picked, _ = _top3_lanes(scores, valid)
        keep = jnp.logical_or(jnp.logical_and(picked, valid), blk == own)
        bias = jnp.where(keep, 0.0, NEG_INF)
        qh = qs if lo else pltpu.roll(qs, DH_A, 1)
        qaug_ref[hh] = jnp.where(lane < DH_A, qh * scale, bias).astype(BF16)


def _select_prompt(qa, kmean_ext):
    rows = qa.shape[0]
    tm = ROW_TILE
    return pl.pallas_call(
        _select_prompt_kernel,
        grid=(rows // tm,),
        in_specs=[pl.BlockSpec((tm, 512), lambda i: (i, 0)),
                  pl.BlockSpec(kmean_ext.shape, lambda i: (0, 0))],
        out_specs=pl.BlockSpec((H_A, tm, LANES), lambda i: (0, i, 0)),
        out_shape=jax.ShapeDtypeStruct((H_A, rows, LANES), BF16),
        compiler_params=_cparams(("arbitrary",)),
        name="moba_select_prompt",
    )(qa, kmean_ext)


def _softmax_step(s, v, m_sc, acc_sc):
    reps = s.shape[1] // LANES
    m_prev = m_sc[...]
    m_next = jnp.maximum(m_prev, jnp.max(s, axis=1, keepdims=True))
    alpha = jnp.exp(m_prev - m_next)
    p = jnp.exp(s - jnp.tile(m_next, (1, reps)))
    acc = acc_sc[...]
    acc_sc[...] = acc * jnp.tile(alpha, (1, acc.shape[1] // LANES)) + _dot(p.astype(BF16), v)
    m_sc[...] = m_next


def _moba_prompt_kernel(q_ref, k_ref, v_ref, o_ref, m_sc, acc_sc):
    tq = q_ref.shape[1]
    own = pl.program_id(1)
    q = q_ref[0]
    m_sc[...] = jnp.full(m_sc.shape, -jnp.inf, F32)
    acc_sc[...] = jnp.zeros(acc_sc.shape, F32)

    def tile(j, causal):
        start = pl.multiple_of(j * MOBA_BLOCK, MOBA_BLOCK)
        k = k_ref[0, pl.ds(start, MOBA_BLOCK), :]
        v = v_ref[0, pl.ds(start, MOBA_BLOCK), :]
        s = _dot_nt(q, k)
        if causal:
            r = lax.broadcasted_iota(jnp.int32, s.shape, 0)
            c = lax.broadcasted_iota(jnp.int32, s.shape, 1)
            s = jnp.where(c <= r, s, NEG_INF)
        _softmax_step(s, v, m_sc, acc_sc)

    tile(own, True)

    def body(j, carry):
        tile(j, False)
        return carry

    lax.fori_loop(0, own, body, 0)
    acc = acc_sc[...]
    o_ref[0] = (acc / acc[:, ONES_LANE_A:ONES_LANE_A + 1]).astype(o_ref.dtype)


def _moba_prompt(qaug, kaug, vaug):
    _, rows, _ = qaug.shape
    tq = MOBA_BLOCK
    return pl.pallas_call(
        _moba_prompt_kernel,
        grid=(H_A, rows // tq),
        in_specs=[pl.BlockSpec((1, tq, LANES), lambda h, i: (h, i, 0)),
                  pl.BlockSpec((1, rows, LANES), lambda h, i: (h, 0, 0)),
                  pl.BlockSpec((1, rows, LANES), lambda h, i: (h, 0, 0))],
        out_specs=pl.BlockSpec((1, tq, LANES), lambda h, i: (h, i, 0)),
        out_shape=jax.ShapeDtypeStruct((H_A, rows, LANES), BF16),
        scratch_shapes=[pltpu.VMEM((tq, LANES), F32), pltpu.VMEM((tq, LANES), F32)],
        compiler_params=_cparams(("arbitrary", "arbitrary")),
        name="moba_attn_prompt",
    )(qaug, kaug, vaug)


def _mla_prompt_kernel(q_ref, k_ref, o_ref, m_sc, acc_sc):
    nh, tq, dk = q_ref.shape
    i = pl.program_id(0)
    q = q_ref[...].reshape(nh * tq, dk)
    m_sc[...] = jnp.full(m_sc.shape, -jnp.inf, F32)
    acc_sc[...] = jnp.zeros(acc_sc.shape, F32)

    def tile(j, causal):
        start = pl.multiple_of(j * tq, tq)
        k = k_ref[pl.ds(start, tq), :]
        s = _dot_nt(q, k)
        if causal:
            s3 = s.reshape(nh, tq, tq)
            r = lax.broadcasted_iota(jnp.int32, s3.shape, 1)
            c = lax.broadcasted_iota(jnp.int32, s3.shape, 2)
            s = jnp.where(c <= r, s3, NEG_INF).reshape(nh * tq, tq)
        _softmax_step(s, k, m_sc, acc_sc)

    tile(i, True)

    def body(j, carry):
        tile(j, False)
        return carry

    lax.fori_loop(0, i, body, 0)
    acc = acc_sc[...]
    lane_l = LANES + ONES_LANE_B
    o = acc[:, :KV_RANK] / acc[:, lane_l:lane_l + 1]
    for hh in range(nh):
        o_ref[:, hh * KV_RANK:(hh + 1) * KV_RANK] = o[hh * tq:(hh + 1) * tq, :].astype(o_ref.dtype)


def _mla_prompt(qcat, kcat):
    nh, rows, dk = qcat.shape
    tq = ROW_TILE
    return pl.pallas_call(
        _mla_prompt_kernel,
        grid=(rows // tq,),
        in_specs=[pl.BlockSpec((nh, tq, dk), lambda i: (0, i, 0)),
                  pl.BlockSpec((rows, dk), lambda i: (0, 0))],
        out_specs=pl.BlockSpec((tq, nh * KV_RANK), lambda i: (i, 0)),
        out_shape=jax.ShapeDtypeStruct((rows, nh * KV_RANK), BF16),
        scratch_shapes=[pltpu.VMEM((nh * tq, LANES), F32), pltpu.VMEM((nh * tq, dk), F32)],
        compiler_params=_cparams(("arbitrary",)),
        name="mla_attn_prompt",
    )(qcat, kcat)


def _merge_kernel(oa_ref, ob_ref, ga_ref, gb_ref, x_ref, gt_ref, g_ref, wpa_ref, wuv_ref, wpb_ref,
                  wout_ref, x1_ref):
    y_a = _dot(oa_ref[0], wpa_ref[0])
    for hh in range(1, H_A):
        y_a = y_a + _dot(oa_ref[hh], wpa_ref[hh])
    o_b = _dot(ob_ref[...], wuv_ref[...])
    y_b = _dot(o_b.astype(BF16), wpb_ref[...])
    merged = jax.nn.sigmoid(ga_ref[...]) * y_a + jax.nn.sigmoid(gb_ref[...]) * y_b
    m = _dot(merged.astype(BF16), wout_ref[...])
    x1_ref[...] = x_ref[...] + gt_ref[...] * _rms(m, g_ref[...])


def _merge(oa, ob, ga, gb, x2, gt1, g_post, wpa_pad, wuv_bd, wpb, wout):
    rows = x2.shape[0]
    tm = ROW_TILE
    row = lambda w: pl.BlockSpec((tm, w), lambda i: (i, 0))
    const = lambda a: pl.BlockSpec(a.shape, lambda i: (0,) * a.ndim)
    mod = const(gt1) if gt1.shape[0] == 1 else row(D_MODEL)
    return pl.pallas_call(
        _merge_kernel,
        grid=(rows // tm,),
        in_specs=[pl.BlockSpec((H_A, tm, LANES), lambda i: (0, i, 0)), row(H_B * KV_RANK),
                  row(D_MODEL), row(D_MODEL), row(D_MODEL), mod, const(g_post), const(wpa_pad),
                  const(wuv_bd), const(wpb), const(wout)],
        out_specs=row(D_MODEL),
        out_shape=jax.ShapeDtypeStruct((rows, D_MODEL), F32),
        compiler_params=_cparams(("arbitrary",)),
        name="mixer_merge",
    )(oa, ob, ga, gb, x2, gt1, g_post, wpa_pad, wuv_bd, wpb, wout)


def _mlp_kernel(x1_ref, sc_ref, sh_ref, gt_ref, gpre_ref, gpost_ref, wup_ref, wdn_ref, y_ref):
    x1 = x1_ref[...]
    h2 = _rms(x1, gpre_ref[...]) * (1.0 + sc_ref[...]) + sh_ref[...]
    u = jnp.maximum(_dot(h2.astype(BF16), wup_ref[...]), 0.0)
    d = _dot((u * u).astype(BF16), wdn_ref[...])
    y_ref[...] = x1 + gt_ref[...] * _rms(d, gpost_ref[...])


def _mlp(x1, sc2, sh2, gt2, g_pre, g_post, wup, wdn):
    rows = x1.shape[0]
    tm = ROW_TILE
    row = lambda w: pl.BlockSpec((tm, w), lambda i: (i, 0))
    const = lambda a: pl.BlockSpec(a.shape, lambda i: (0,) * a.ndim)
    mod = (lambda a: const(a)) if sc2.shape[0] == 1 else (lambda a: row(D_MODEL))
    return pl.pallas_call(
        _mlp_kernel,
        grid=(rows // tm,),
        in_specs=[row(D_MODEL), mod(sc2), mod(sh2), mod(gt2), const(g_pre), const(g_post),
                  const(wup), const(wdn)],
        out_specs=row(D_MODEL),
        out_shape=jax.ShapeDtypeStruct((rows, D_MODEL), F32),
        compiler_params=_cparams(("arbitrary",)),
        name="sqrelu_mlp",
    )(x1, sc2, sh2, gt2, g_pre, g_post, wup, wdn)


def _ring_step(step, total, copies):
    @pl.when(step == 0)
    def _():
        for cp in copies(step, 0):
            cp.start()

    @pl.when(step + 1 < total)
    def _():
        for cp in copies(step + 1, (step + 1) % 2):
            cp.start()

    for cp in copies(step, step % 2):
        cp.wait()


def _select_sample_kernel(pc, pt_ref, q_ref, ck_ref, sel_ref, buf, kmt, sem):
    step = pl.program_id(0)
    n_pages = pt_ref.shape[1]
    nch = n_pages // pc
    nb = n_pages // 2

    def copies(st, slot):
        b, c = st // nch, st % nch
        return [pltpu.make_async_copy(ck_ref.at[pt_ref[b, c * pc + p]], buf.at[slot, p], sem.at[slot])
                for p in range(pc)]

    _ring_step(step, pl.num_programs(0), copies)
    c = step % nch
    slot = step % 2

    @pl.when(c == 0)
    def _():
        kmt[...] = jnp.zeros(kmt.shape, F32)

    lane = _lane_iota((DH_A, LANES))
    for hh in range(H_A):
        rows = slice((hh % 2) * DH_A, (hh % 2 + 1) * DH_A)
        acc = kmt[hh // 2, rows, :]
        for j in range(pc // 2):
            pair = buf[slot, 2 * j, hh] + buf[slot, 2 * j + 1, hh]
            col = jnp.sum(pair, axis=1, keepdims=True) * (1.0 / MOBA_BLOCK)
            acc = jnp.where(lane == c * (pc // 2) + j, col, acc)
        kmt[hh // 2, rows, :] = acc

    @pl.when(c == nch - 1)
    def _():
        q = q_ref[0]
        rows = q.shape[0]
        lane_q = _lane_iota((rows, LANES))
        out = jnp.zeros((rows, LANES), jnp.int32)
        for hh in range(H_A):
            sl = slice((hh // 2) * LANES, (hh // 2 + 1) * LANES)
            half_mask = (lane_q < DH_A) if hh % 2 == 0 else (lane_q >= DH_A)
            scores = jnp.dot(jnp.where(half_mask, q[:, sl], 0.0), kmt[hh // 2],
                             precision=lax.Precision.HIGHEST, preferred_element_type=F32)
            _, idxs = _top3_lanes(scores, lane_q < nb)
            for r, idx in enumerate(idxs):
                out = jnp.where(lane_q == hh * MOBA_TOPK + r, idx, out)
        sel_ref[0] = out


def _select_sample(page_table, q_pad, ck_t):
    db, n_pages = page_table.shape
    rows = q_pad.shape[1]
    pc = next(c for c in (16, 8, 4, 2) if n_pages % c == 0)
    nch = n_pages // pc
    return pl.pallas_call(
        functools.partial(_select_sample_kernel, pc),
        grid_spec=pltpu.PrefetchScalarGridSpec(
            num_scalar_prefetch=1, grid=(db * nch,),
            in_specs=[pl.BlockSpec((1, rows, 512), lambda s, pt: (s // nch, 0, 0)),
                      pl.BlockSpec(memory_space=pl.ANY)],
            out_specs=pl.BlockSpec((1, rows, LANES), lambda s, pt: (s // nch, 0, 0)),
            scratch_shapes=[pltpu.VMEM((2, pc, H_A, DH_A, PAGE_SIZE), F32),
                            pltpu.VMEM((H_A // 2, 2 * DH_A, LANES), F32),
                            pltpu.SemaphoreType.DMA((2,))]),
        out_shape=jax.ShapeDtypeStruct((db, rows, LANES), jnp.int32),
        compiler_params=_cparams(("arbitrary",)),
        name="moba_select_sample",
    )(page_table, q_pad, ck_t)


def _moba_sample_kernel(n_tok, pt_ref, sel_ref, q_ref, kn_ref, vn_ref, ck_ref, cv_ref, o_ref,
                        kbuf, vbuf, acc, sem):
    step = pl.program_id(0)
    t = step % n_tok

    def copies(st, slot):
        b, tt = st // n_tok, st % n_tok
        out = []
        for hh in range(H_A):
            for n in range(MOBA_TOPK):
                blk = sel_ref[b, tt * (H_A * MOBA_TOPK) + hh * MOBA_TOPK + n]
                for pg in range(2):
                    phys = pt_ref[b, blk * 2 + pg]
                    dst = pl.ds((n * 2 + pg) * PAGE_SIZE, PAGE_SIZE)
                    out.append(pltpu.make_async_copy(ck_ref.at[phys, hh], kbuf.at[slot, hh, :, dst],
                                                     sem.at[0, slot]))
                    out.append(pltpu.make_async_copy(cv_ref.at[phys, hh], vbuf.at[slot, hh, :, dst],
                                                     sem.at[1, slot]))
        return out

    _ring_step(step, pl.num_programs(0), copies)
    slot = step % 2

    @pl.when(t == 0)
    def _():
        acc[...] = jnp.zeros(acc.shape, F32)

    scale = DH_A ** -0.5
    q = q_ref[0]
    kn = kn_ref[0]
    vn = vn_ref[0]
    rows = q.shape[0]
    r_i = lax.broadcasted_iota(jnp.int32, (rows, rows), 0)
    c_i = lax.broadcasted_iota(jnp.int32, (rows, rows), 1)
    own_ok = jnp.logical_and(c_i <= r_i, c_i < n_tok)
    row_is_t = lax.broadcasted_iota(jnp.int32, (rows, DH_A), 0) == t
    for hh in range(H_A):
        sl = slice(hh * DH_A, (hh + 1) * DH_A)
        qh = (q[:, sl] * scale).astype(BF16)
        ks_t = kbuf[slot, hh].astype(BF16)
        vs_t = vbuf[slot, hh].astype(BF16)
        s = _dot(qh, ks_t)
        s_own = jnp.where(own_ok, _dot_nt(qh, kn[:, sl].astype(BF16)), NEG_INF)
        m = jnp.maximum(jnp.max(s, axis=1, keepdims=True), jnp.max(s_own, axis=1, keepdims=True))
        p = jnp.exp(s - m)
        p_own = jnp.exp(s_own - m)
        l = jnp.sum(p, axis=1, keepdims=True) + jnp.sum(p_own, axis=1, keepdims=True)
        o = (_dot_nt(p.astype(BF16), vs_t) + _dot(p_own.astype(BF16), vn[:, sl].astype(BF16))) / l
        acc[:, sl] = jnp.where(row_is_t, o, acc[:, sl])
    o_ref[0] = acc[...]


def _moba_sample(page_table, sel, q_pad, kn_pad, vn_pad, ck_t, cv_t, n_tok):
    db, rows, _ = q_pad.shape
    n_keys = MOBA_TOPK * 2 * PAGE_SIZE
    tok = lambda: pl.BlockSpec((1, rows, 512), lambda s, pt, sl: (s // n_tok, 0, 0))
    return pl.pallas_call(
        functools.partial(_moba_sample_kernel, n_tok),
        grid_spec=pltpu.PrefetchScalarGridSpec(
            num_scalar_prefetch=2, grid=(db * n_tok,),
            in_specs=[tok(), tok(), tok(), pl.BlockSpec(memory_space=pl.ANY),
                      pl.BlockSpec(memory_space=pl.ANY)],
            out_specs=tok(),
            scratch_shapes=[pltpu.VMEM((2, H_A, DH_A, n_keys), F32),
                            pltpu.VMEM((2, H_A, DH_A, n_keys), F32),
                            pltpu.VMEM((rows, 512), F32), pltpu.SemaphoreType.DMA((2, 2))]),
        out_shape=jax.ShapeDtypeStruct((db, rows, 512), F32),
        compiler_params=_cparams(("arbitrary",)),
        name="moba_attn_sample",
    )(page_table, sel, q_pad, kn_pad, vn_pad, ck_t, cv_t)


def _mla_sample_kernel(n_tok, pages, pt_ref, q_ref, kn_ref, cl_ref, cr_ref, o_ref,
                       lbuf, rbuf, m_sc, l_sc, acc_sc, sem):
    step = pl.program_id(0)
    nc = pt_ref.shape[1] // pages
    c = step % nc

    def copies(st, slot):
        b, cc = st // nc, st % nc
        out = []
        for p in range(pages):
            phys = pt_ref[b, cc * pages + p]
            dst = pl.ds(p * PAGE_SIZE, PAGE_SIZE)
            out.append(pltpu.make_async_copy(cl_ref.at[phys], lbuf.at[slot, dst, :], sem.at[0, slot]))
            out.append(pltpu.make_async_copy(cr_ref.at[phys], rbuf.at[slot, :, dst], sem.at[1, slot]))
        return out

    _ring_step(step, pl.num_programs(0), copies)
    slot = step % 2

    @pl.when(c == 0)
    def _():
        m_sc[...] = jnp.full(m_sc.shape, -jnp.inf, F32)
        l_sc[...] = jnp.zeros(l_sc.shape, F32)
        acc_sc[...] = jnp.zeros(acc_sc.shape, F32)

    q = q_ref[0]
    q_lat, q_rope = q[:, :KV_RANK], q[:, KV_RANK:KV_RANK + ROPE_B]

    def update(s, v):
        m_prev = m_sc[...]
        m_next = jnp.maximum(m_prev, jnp.max(s, axis=1, keepdims=True))
        alpha = jnp.exp(m_prev - m_next)
        p = jnp.exp(s - m_next[:, :1])
        l_sc[...] = l_sc[...] * alpha + jnp.sum(p, axis=1, keepdims=True)
        acc_sc[...] = acc_sc[...] * alpha + _dot(p.astype(BF16), v)
        m_sc[...] = m_next

    lat = lbuf[slot].astype(BF16)
    s = _dot_nt(q_lat, lat) + _dot(q_rope, rbuf[slot].astype(BF16))
    update(s, lat)

    @pl.when(c == nc - 1)
    def _():
        kn = kn_ref[0]
        rows, n_new = q.shape[0], kn.shape[0]
        s_new = _dot_nt(q_lat, kn[:, :KV_RANK]) + _dot_nt(q_rope, kn[:, KV_RANK:KV_RANK + ROPE_B])
        tok = lax.broadcasted_iota(jnp.int32, (rows, n_new), 0) // H_B
        col = lax.broadcasted_iota(jnp.int32, (rows, n_new), 1)
        s_new = jnp.where(jnp.logical_and(col <= tok, col < n_tok), s_new, NEG_INF)
        update(s_new, kn[:, :KV_RANK])
        o_ref[0] = (acc_sc[...] / l_sc[...]).astype(o_ref.dtype)


def _mla_sample(page_table, q_rows, kn_pad, cache_latent, ckr_t, n_tok):
    db, rows, dk = q_rows.shape
    n_pages = page_table.shape[1]
    pages = next(c for c in (32, 16, 8, 4, 2, 1) if n_pages % c == 0)
    nc = n_pages // pages
    return pl.pallas_call(
        functools.partial(_mla_sample_kernel, n_tok, pages),
        grid_spec=pltpu.PrefetchScalarGridSpec(
            num_scalar_prefetch=1, grid=(db * nc,),
            in_specs=[pl.BlockSpec((1, rows, dk), lambda s, pt: (s // nc, 0, 0)),
                      pl.BlockSpec((1,) + kn_pad.shape[1:], lambda s, pt: (s // nc, 0, 0)),
                      pl.BlockSpec(memory_space=pl.ANY), pl.BlockSpec(memory_space=pl.ANY)],
            out_specs=pl.BlockSpec((1, rows, KV_RANK), lambda s, pt: (s // nc, 0, 0)),
            scratch_shapes=[pltpu.VMEM((2, pages * PAGE_SIZE, KV_RANK), F32),
                            pltpu.VMEM((2, ROPE_B, pages * PAGE_SIZE), F32),
                            pltpu.VMEM((rows, LANES), F32), pltpu.VMEM((rows, LANES), F32),
                            pltpu.VMEM((rows, KV_RANK), F32), pltpu.SemaphoreType.DMA((2, 2))]),
        out_shape=jax.ShapeDtypeStruct((db, rows, KV_RANK), BF16),
        compiler_params=_cparams(("arbitrary",)),
        name="mla_attn_sample",
    )(page_table, q_rows, kn_pad, cache_latent, ckr_t)


def _prepare_weights(w_in, w_uq, w_uk, w_uv, w_proj_a, w_proj_b, w_out, w_up, w_down):
    o = 3 * H_A * DH_A
    segs = [w_in[:, :o + Q_RANK + KV_RANK], w_in[:, o + Q_RANK + KV_RANK + ROPE_B:],
            w_in[:, o + Q_RANK + KV_RANK:o + Q_RANK + KV_RANK + ROPE_B],
            jnp.zeros((D_MODEL, LANES - ROPE_B), w_in.dtype)]
    w_in_p = jnp.concatenate(segs, axis=1).astype(BF16)
    wq = w_uq.reshape(Q_RANK, H_B, NOPE_B + ROPE_B)
    wuq_n = wq[:, :, :NOPE_B].reshape(Q_RANK, H_B * NOPE_B).astype(BF16)
    wuq_r = jnp.pad(wq[:, :, NOPE_B:], ((0, 0), (0, 0), (0, LANES - ROPE_B)))
    wuq_r = wuq_r.reshape(Q_RANK, H_B * LANES).astype(BF16)
    eye = jnp.eye(H_B, dtype=w_uk.dtype)
    wuk_bd = jnp.einsum('rhn,hg->hngr', w_uk, eye).reshape(H_B * NOPE_B, H_B * KV_RANK).astype(BF16)
    wuv_bd = jnp.einsum('rhv,hg->hrgv', w_uv, eye).reshape(H_B * KV_RANK, H_B * V_B).astype(BF16)
    wpa_pad = jnp.pad(w_proj_a.reshape(H_A, DH_A, D_MODEL), ((0, 0), (0, LANES - DH_A), (0, 0)))
    return dict(w_in_p=w_in_p, wuq_n=wuq_n, wuq_r=wuq_r, wuk_bd=wuk_bd, wuv_bd=wuv_bd,
                wpa_pad=wpa_pad.astype(BF16), wpb=w_proj_b.astype(BF16), wout=w_out.astype(BF16),
                wup=w_up.astype(BF16), wdn=w_down.astype(BF16))


def _row(g):
    return g.reshape(1, -1)


def _trunk(x2, mods, pos, attend, prompt, wts, g_pre_mix, g_post_mix, g_pre_mlp, g_post_mlp,
           g_q_lat, g_kv_lat):
    sh1, sc1, gt1, sh2, sc2, gt2 = mods
    tabs_a = _rope_tables(pos, ROT_A, THETA_A, DH_A)
    tabs_b = _rope_tables(pos, ROPE_B, THETA_B, ROPE_B)
    outs = _proj_in(x2, sc1, sh1, _row(g_pre_mix), wts['w_in_p'], _row(g_q_lat), _row(g_kv_lat),
                    wts['wuq_n'], wts['wuq_r'], wts['wuk_bd'], tabs_a, tabs_b, prompt)
    qa, k, v, latent, krope, ga, gb, qcat, kcat = outs[:9]
    oa, ob = attend(qa, k, v, qcat, kcat, outs[9:])
    x1 = _merge(oa, ob, ga, gb, x2, gt1, _row(g_post_mix), wts['wpa_pad'], wts['wuv_bd'],
                wts['wpb'], wts['wout'])
    y = _mlp(x1, sc2, sh2, gt2, _row(g_pre_mlp), _row(g_post_mlp), wts['wup'], wts['wdn'])
    return y, k, v, latent, krope


def kernel(x_prompt, x_sample, c_prompt, c_sample, cache_k, cache_v, cache_latent, cache_krope, page_table, w_ada, b_ada, g_pre_mix, g_post_mix, g_pre_mlp, g_post_mlp, w_in, g_q_lat, g_kv_lat, w_uq, w_uk, w_uv, w_proj_a, w_proj_b, w_out, w_up, w_down):
    bsz, seq, _ = x_prompt.shape
    db, n_tok, _ = x_sample.shape
    n_pages = page_table.shape[1]
    past = n_pages * PAGE_SIZE
    assert bsz == 1 and seq % ROW_TILE == 0 and seq // MOBA_BLOCK <= N_SEL
    assert (db * n_tok) % ROW_TILE == 0 and n_tok <= 8
    assert past % MOBA_BLOCK == 0 and MOBA_TOPK <= past // MOBA_BLOCK <= LANES

    wts = _prepare_weights(w_in, w_uq, w_uk, w_uv, w_proj_a, w_proj_b, w_out, w_up, w_down)
    gains = (g_pre_mix, g_post_mix, g_pre_mlp, g_post_mlp, g_q_lat, g_kv_lat)

    n_c = bsz + db
    c_all = jnp.pad(jnp.concatenate([c_prompt, c_sample], axis=0), ((0, -n_c % 8), (0, 0)))
    mod_all = _modulation(c_all, w_ada, b_ada)
    mods_p = tuple(jnp.split(mod_all[:bsz], 6, axis=1))
    mods_s = tuple(jnp.split(jnp.repeat(mod_all[bsz:n_c], n_tok, axis=0), 6, axis=1))

    def attend_prompt(qa, k, v, qcat, kcat, extra):
        kaug, vaug, kmean = extra
        nb = kmean.shape[0]
        kmean_ext = jnp.pad(kmean.reshape(nb, 512), ((N_SEL, LANES - N_SEL - nb), (0, 0)))
        qaug = _select_prompt(qa, kmean_ext)
        return _moba_prompt(qaug, kaug, vaug), _mla_prompt(qcat, kcat)

    pos_p = jnp.arange(seq, dtype=jnp.int32)
    y_p, k_p, v_p, lat_p, kr_p = _trunk(x_prompt.reshape(seq, D_MODEL), mods_p, pos_p, attend_prompt,
                                        True, wts, *gains)

    rows_pad = 8
    ck_t = cache_k.transpose(0, 2, 3, 1)
    cv_t = cache_v.transpose(0, 2, 3, 1)
    ckr_t = cache_krope.transpose(0, 2, 1)

    def attend_sample(qa, k, v, qcat, kcat, extra):
        pad_tok = lambda a: jnp.pad(a.reshape(db, n_tok, a.shape[-1]),
                                    ((0, 0), (0, rows_pad - n_tok), (0, 0)))
        q_pad, kn_pad, vn_pad = pad_tok(qa), pad_tok(k), pad_tok(v)
        sel = _select_sample(page_table, q_pad, ck_t)
        sel = sel[:, :n_tok, :H_A * MOBA_TOPK].reshape(db, n_tok * H_A * MOBA_TOPK)
        oa = _moba_sample(page_table, sel, q_pad, kn_pad, vn_pad, ck_t, cv_t, n_tok)
        oa = oa[:, :n_tok].reshape(db * n_tok, H_A, DH_A).transpose(1, 0, 2)
        oa = jnp.pad(oa, ((0, 0), (0, 0), (0, LANES - DH_A))).astype(BF16)
        q_rows = qcat.reshape(H_B, db, n_tok, 2 * LANES).transpose(1, 2, 0, 3)
        q_rows = q_rows.reshape(db, n_tok * H_B, 2 * LANES)
        ob = _mla_sample(page_table, q_rows, pad_tok(kcat), cache_latent, ckr_t, n_tok)
        return oa, ob.reshape(db * n_tok, H_B * KV_RANK)

    pos_s = past + jnp.tile(jnp.arange(n_tok, dtype=jnp.int32), db)
    y_s, k_s, v_s, lat_s, kr_s = _trunk(x_sample.reshape(db * n_tok, D_MODEL), mods_s, pos_s,
                                        attend_sample, False, wts, *gains)

    return (y_p.reshape(bsz, seq, D_MODEL), y_s.reshape(db, n_tok, D_MODEL),
            k_p.reshape(bsz, seq, H_A, DH_A), v_p.reshape(bsz, seq, H_A, DH_A),
            lat_p.reshape(bsz, seq, KV_RANK), kr_p.reshape(bsz, seq, ROPE_B),
            k_s.reshape(db, n_tok, H_A, DH_A), v_s.reshape(db, n_tok, H_A, DH_A),
            lat_s.reshape(db, n_tok, KV_RANK), kr_s.reshape(db, n_tok, ROPE_B))
```

```python
import functools

import jax
import jax.numpy as jnp
from jax import lax
from jax.experimental import pallas as pl
from jax.experimental.pallas import tpu as pltpu

F32 = jnp.float32
BF16 = jnp.bfloat16

D_MODEL = 1024
H_A = 8
DH_A = 64
ROT_A = DH_A // 4
THETA_A = 500000.0
MOBA_BLOCK = 256
MOBA_TOPK = 3
H_B = 8
Q_RANK = 256
KV_RANK = 128
NOPE_B = 64
ROPE_B = 32
V_B = 64
THETA_B = 10000.0
D_FF = 4 * D_MODEL
PAGE_SIZE = 128
EPS = 1e-6
NEG_INF = -1e30

LANES = 128
ROW_TILE = 256
N_SEL = 64
VMEM_LIMIT = 56 * 1024 * 1024

C_QA, C_KA, C_VA = 0, 512, 1024
C_QLAT, C_KVLAT = 1536, 1792
C_GA, C_GB, C_KR = 1920, 2944, 3968
D_IN_PAD = 4096
ONES_LANE_A = DH_A
ONES_LANE_B = ROPE_B


def _cparams(sem, vmem=VMEM_LIMIT):
    return pltpu.CompilerParams(dimension_semantics=sem, vmem_limit_bytes=vmem)


def _rms(x, g):
    return x * lax.rsqrt(jnp.mean(x * x, axis=-1, keepdims=True) + EPS) * g


def _dot(a, b):
    return jnp.dot(a, b, preferred_element_type=F32)


def _dot_nt(a, b, precision=None):
    return lax.dot_general(a, b, (((1,), (1,)), ((), ())), precision=precision,
                           preferred_element_type=F32)


def _lane_iota(shape):
    return lax.broadcasted_iota(jnp.int32, shape, len(shape) - 1)


def _rope_slab(x, c, s_up, s_dn, half):
    return x * c + pltpu.roll(x, half, 1) * s_up + pltpu.roll(x, LANES - half, 1) * s_dn


def _rope_tables(pos, rot, theta, group):
    half = rot // 2
    inv = 1.0 / (theta ** (jnp.arange(half, dtype=F32) / half))
    ang = pos.astype(F32)[:, None] * inv[None, :]
    cos, sin = jnp.cos(ang), jnp.sin(ang)
    n = pos.shape[0]
    z_half = jnp.zeros((n, half), F32)
    rest1 = jnp.ones((n, group - rot), F32)
    rest0 = jnp.zeros((n, group - rot), F32)
    c = jnp.concatenate([cos, cos, rest1], axis=1)
    s_up = jnp.concatenate([z_half, sin, rest0], axis=1)
    s_dn = jnp.concatenate([-sin, z_half, rest0], axis=1)
    rep = LANES // group
    return tuple(jnp.tile(t, (1, rep)) for t in (c, s_up, s_dn))


def _mod_kernel(c_ref, w_ref, b_ref, o_ref):
    c = c_ref[...]
    a = (c * jax.nn.sigmoid(c)).astype(BF16)
    o_ref[...] = _dot(a, w_ref[...].astype(BF16)) + b_ref[...]


def _modulation(c_all, w_ada, b_ada):
    rows = c_all.shape[0]
    n = w_ada.shape[1]
    tn = 768
    return pl.pallas_call(
        _mod_kernel,
        grid=(n // tn,),
        in_specs=[pl.BlockSpec((rows, D_MODEL), lambda j: (0, 0)),
                  pl.BlockSpec((D_MODEL, tn), lambda j: (0, j)),
                  pl.BlockSpec((1, tn), lambda j: (0, j))],
        out_specs=pl.BlockSpec((rows, tn), lambda j: (0, j)),
        out_shape=jax.ShapeDtypeStruct((rows, n), F32),
        compiler_params=_cparams(("arbitrary",)),
        name="adaln_modulation",
    )(c_all, w_ada, b_ada.reshape(1, n))


def _proj_in_kernel(prompt, x_ref, sc_ref, sh_ref, g_ref, w_ref, gq_ref, gkv_ref, wuqn_ref, wuqr_ref,
                    wuk_ref, ca_ref, sau_ref, sad_ref, cb_ref, sbu_ref, sbd_ref,
                    qa_ref, k_ref, v_ref, lat_ref, kr_ref, ga_ref, gb_ref, qcat_ref, kcat_ref,
                    *moba_refs):
    tm = x_ref.shape[0]
    x = x_ref[...]
    h = _rms(x, g_ref[...]) * (1.0 + sc_ref[...]) + sh_ref[...]
    p = _dot(h.astype(BF16), w_ref[...])

    ca, sau, sad = ca_ref[...], sau_ref[...], sad_ref[...]
    cb, sbu, sbd = cb_ref[...], sbu_ref[...], sbd_ref[...]

    def rope_a(y):
        return jnp.concatenate(
            [_rope_slab(y[:, s * LANES:(s + 1) * LANES], ca, sau, sad, ROT_A // 2)
             for s in range(H_A * DH_A // LANES)], axis=1)

    qa = rope_a(p[:, C_QA:C_QA + 512])
    ka = rope_a(p[:, C_KA:C_KA + 512])
    va = p[:, C_VA:C_VA + 512]
    qa_ref[...] = qa
    k_ref[...] = ka
    v_ref[...] = va
    ga_ref[...] = p[:, C_GA:C_GA + D_MODEL]
    gb_ref[...] = p[:, C_GB:C_GB + D_MODEL]

    latent = _rms(p[:, C_KVLAT:C_KVLAT + KV_RANK], gkv_ref[...])
    lat_ref[...] = latent
    kr = _rope_slab(p[:, C_KR:C_KR + LANES], cb, sbu, sbd, ROPE_B // 2)
    kr_ref[...] = kr[:, :ROPE_B]
    lane = _lane_iota((tm, LANES))
    kr_slab = jnp.where(lane == ONES_LANE_B, 1.0, kr)
    kcat_ref[...] = jnp.concatenate([latent, kr_slab], axis=1).astype(BF16)

    scale_b = (NOPE_B + ROPE_B) ** -0.5
    qn = _rms(p[:, C_QLAT:C_QLAT + Q_RANK], gq_ref[...]).astype(BF16)
    q_nope = _dot(qn, wuqn_ref[...])
    q_rope = _dot(qn, wuqr_ref[...])
    q_abs = _dot(q_nope.astype(BF16), wuk_ref[...])
    for hh in range(H_B):
        sl = slice(hh * LANES, (hh + 1) * LANES)
        qr = _rope_slab(q_rope[:, sl], cb, sbu, sbd, ROPE_B // 2)
        qcat_ref[hh] = (jnp.concatenate([q_abs[:, sl], qr], axis=1) * scale_b).astype(BF16)

    if prompt:
        kaug_ref, vaug_ref, kmean_ref = moba_refs
        blk = pl.program_id(0)
        onehot = jnp.where(lane == N_SEL + blk, 1.0, 0.0)
        ones_col = jnp.where(lane == ONES_LANE_A, 1.0, 0.0)
        for hh in range(H_A):
            sl = slice((hh // 2) * LANES, (hh // 2 + 1) * LANES)
            ks, vs = ka[:, sl], va[:, sl]
            if hh % 2 == 1:
                ks = pltpu.roll(ks, DH_A, 1)
                vs = pltpu.roll(vs, DH_A, 1)
            kaug_ref[hh] = jnp.where(lane < DH_A, ks, onehot).astype(BF16)
            vaug_ref[hh] = jnp.where(lane < DH_A, vs, ones_col).astype(BF16)
        kmean_ref[0] = jnp.sum(ka, axis=0, keepdims=True) * (1.0 / MOBA_BLOCK)


def _proj_in(x2, sc, sh, g_pre, w_in_p, g_q, g_kv, wuq_n, wuq_r, wuk_bd, tabs_a, tabs_b, prompt):
    rows = x2.shape[0]
    tm = ROW_TILE
    nt = rows // tm
    row = lambda w: pl.BlockSpec((tm, w), lambda i: (i, 0))
    const = lambda a: pl.BlockSpec(a.shape, lambda i: (0,) * a.ndim)
    mod = (lambda a: const(a)) if sc.shape[0] == 1 else (lambda a: row(a.shape[1]))
    heads = lambda w: pl.BlockSpec((H_A, tm, w), lambda i: (0, i, 0))
    in_specs = [row(D_MODEL), mod(sc), mod(sh), const(g_pre), const(w_in_p), const(g_q), const(g_kv),
                const(wuq_n), const(wuq_r), const(wuk_bd)] + [row(LANES)] * 6
    out_shape = [jax.ShapeDtypeStruct((rows, 512), F32)] * 3 + [
        jax.ShapeDtypeStruct((rows, KV_RANK), F32), jax.ShapeDtypeStruct((rows, ROPE_B), F32),
        jax.ShapeDtypeStruct((rows, D_MODEL), F32), jax.ShapeDtypeStruct((rows, D_MODEL), F32),
        jax.ShapeDtypeStruct((H_B, rows, 2 * LANES), BF16), jax.ShapeDtypeStruct((rows, 2 * LANES), BF16)]
    out_specs = [row(512)] * 3 + [row(KV_RANK), row(ROPE_B), row(D_MODEL), row(D_MODEL),
                                  heads(2 * LANES), row(2 * LANES)]
    if prompt:
        out_shape += [jax.ShapeDtypeStruct((H_A, rows, LANES), BF16)] * 2 + [
            jax.ShapeDtypeStruct((nt, 1, 512), F32)]
        out_specs += [heads(LANES), heads(LANES), pl.BlockSpec((1, 1, 512), lambda i: (i, 0, 0))]
    return pl.pallas_call(
        functools.partial(_proj_in_kernel, prompt),
        grid=(nt,), in_specs=in_specs, out_specs=out_specs, out_shape=out_shape,
        compiler_params=_cparams(("arbitrary",)),
        name="proj_in_prompt" if prompt else "proj_in_sample",
    )(x2, sc, sh, g_pre, w_in_p, g_q, g_kv, wuq_n, wuq_r, wuk_bd, *tabs_a, *tabs_b)


def _top3_lanes(scores, valid):
    lane = _lane_iota(scores.shape)
    s = jnp.where(valid, scores, NEG_INF)
    picked = jnp.zeros(scores.shape, jnp.bool_)
    idxs = []
    for _ in range(MOBA_TOPK):
        mx = jnp.max(s, axis=1, keepdims=True)
        idx = jnp.min(jnp.where(s == mx, lane, 2 * LANES), axis=1, keepdims=True)
        hit = lane == idx
        picked = jnp.logical_or(picked, hit)
        s = jnp.where(hit, -jnp.inf, s)
        idxs.append(idx)
    return picked, idxs


def _select_prompt_kernel(qa_ref, kmean_ref, qaug_ref):
    tm = qa_ref.shape[0]
    own = pl.program_id(0)
    lane = _lane_iota((tm, LANES))
    blk = lane - N_SEL
    scale = DH_A ** -0.5
    for hh in range(H_A):
        sl = slice((hh // 2) * LANES, (hh // 2 + 1) * LANES)
        qs = qa_ref[:, sl]
        km = kmean_ref[:, sl]
        lo = (hh % 2) == 0
        half_mask = (lane < DH_A) if lo else (lane >= DH_A)
        scores = _dot_nt(jnp.where(half_mask, qs, 0.0), km, precision=lax.Precision.HIGHEST)
        valid = jnp.logical_and(blk >= 0, blk < own)
        picked, _ = _top3_lanes(scores, valid)
        keep = jnp.logical_or(jnp.logical_and(picked, valid), blk == own)
        bias = jnp.where(keep, 0.0, NEG_INF)
        qh = qs if lo else pltpu.roll(qs, DH_A, 1)
        qaug_ref[hh] = jnp.where(lane < DH_A, qh * scale, bias).astype(BF16)


def _select_prompt(qa, kmean_ext):
    rows = qa.shape[0]
    tm = ROW_TILE
    return pl.pallas_call(
        _select_prompt_kernel,
        grid=(rows // tm,),
        in_specs=[pl.BlockSpec((tm, 512), lambda i: (i, 0)),
                  pl.BlockSpec(kmean_ext.shape, lambda i: (0, 0))],
        out_specs=pl.BlockSpec((H_A, tm, LANES), lambda i: (0, i, 0)),
        out_shape=jax.ShapeDtypeStruct((H_A, rows, LANES), BF16),
        compiler_params=_cparams(("arbitrary",)),
        name="moba_select_prompt",
    )(qa, kmean_ext)


def _softmax_step(s, v, m_sc, acc_sc):
    reps = s.shape[1] // LANES
    m_prev = m_sc[...]
    m_next = jnp.maximum(m_prev, jnp.max(s, axis=1, keepdims=True))
    alpha = jnp.exp(m_prev - m_next)
    p = jnp.exp(s - jnp.tile(m_next, (1, reps)))
    acc = acc_sc[...]
    acc_sc[...] = acc * jnp.tile(alpha, (1, acc.shape[1] // LANES)) + _dot(p.astype(BF16), v)
    m_sc[...] = m_next


def _moba_prompt_kernel(q_ref, k_ref, v_ref, o_ref, sa_sc, sb_sc, m_sc, acc_sc):
    hb = q_ref.shape[0]
    own = pl.program_id(1)
    m_sc[...] = jnp.full(m_sc.shape, -jnp.inf, F32)
    acc_sc[...] = jnp.zeros(acc_sc.shape, F32)

    def rows_of(ref, hh, blk):
        return ref[hh, pl.ds(pl.multiple_of(blk * MOBA_BLOCK, MOBA_BLOCK), MOBA_BLOCK), :]

    def scores(hh, blk):
        return _dot_nt(q_ref[hh], rows_of(k_ref, hh, blk))

    def consume(cur, blk):
        for hh in range(hb):
            _softmax_step(cur[hh], rows_of(v_ref, hh, blk), m_sc.at[hh], acc_sc.at[hh])

    def stage(cur, nxt, blk_next, blk_cur):
        for hh in range(hb):
            nxt[hh] = scores(hh, blk_next)
        consume(cur, blk_cur)

    for hh in range(hb):
        s = scores(hh, own)
        r = lax.broadcasted_iota(jnp.int32, s.shape, 0)
        c = lax.broadcasted_iota(jnp.int32, s.shape, 1)
        sa_sc[hh] = jnp.where(c <= r, s, NEG_INF)

    def body(jj, carry):
        j0 = 2 * jj
        stage(sa_sc, sb_sc, j0, jnp.where(jj == 0, own, j0 - 1))
        stage(sb_sc, sa_sc, j0 + 1, j0)
        return carry

    lax.fori_loop(0, own // 2, body, 0)

    @pl.when(own % 2 == 1)
    def _():
        stage(sa_sc, sb_sc, own - 1, jnp.where(own == 1, own, own - 2))
        consume(sb_sc, own - 1)

    @pl.when(own % 2 == 0)
    def _():
        consume(sa_sc, jnp.maximum(own - 1, 0))

    for hh in range(hb):
        acc = acc_sc[hh]
        o_ref[hh] = (acc / acc[:, ONES_LANE_A:ONES_LANE_A + 1]).astype(o_ref.dtype)


MOBA_HEADS_PER_STEP = 4


def _moba_prompt(qaug, kaug, vaug):
    _, rows, _ = qaug.shape
    tq = MOBA_BLOCK
    hb = MOBA_HEADS_PER_STEP
    resident = lambda: pl.BlockSpec((hb, rows, LANES), lambda g, i: (g, 0, 0),
                                    pipeline_mode=pl.Buffered(1))
    return pl.pallas_call(
        _moba_prompt_kernel,
        grid=(H_A // hb, rows // tq),
        in_specs=[pl.BlockSpec((hb, tq, LANES), lambda g, i: (g, i, 0)), resident(), resident()],
        out_specs=pl.BlockSpec((hb, tq, LANES), lambda g, i: (g, i, 0)),
        out_shape=jax.ShapeDtypeStruct((H_A, rows, LANES), BF16),
        scratch_shapes=[pltpu.VMEM((hb, tq, MOBA_BLOCK), F32), pltpu.VMEM((hb, tq, MOBA_BLOCK), F32),
                        pltpu.VMEM((hb, tq, LANES), F32), pltpu.VMEM((hb, tq, LANES), F32)],
        compiler_params=_cparams(("arbitrary", "arbitrary")),
        name="moba_attn_prompt",
    )(qaug, kaug, vaug)


def _mla_prompt_kernel(q_ref, k_ref, o_ref, m_sc, acc_sc):
    nh, tq, dk = q_ref.shape
    i = pl.program_id(0)
    q = q_ref[...].reshape(nh * tq, dk)
    m_sc[...] = jnp.full(m_sc.shape, -jnp.inf, F32)
    acc_sc[...] = jnp.zeros(acc_sc.shape, F32)

    def tile(j, causal):
        start = pl.multiple_of(j * tq, tq)
        k = k_ref[pl.ds(start, tq), :]
        s = _dot_nt(q, k)
        if causal:
            s3 = s.reshape(nh, tq, tq)
            r = lax.broadcasted_iota(jnp.int32, s3.shape, 1)
            c = lax.broadcasted_iota(jnp.int32, s3.shape, 2)
            s = jnp.where(c <= r, s3, NEG_INF).reshape(nh * tq, tq)
        _softmax_step(s, k, m_sc, acc_sc)

    tile(i, True)

    def body(j, carry):
        tile(j, False)
        return carry

    lax.fori_loop(0, i, body, 0)
    acc = acc_sc[...]
    lane_l = LANES + ONES_LANE_B
    o = acc[:, :KV_RANK] / acc[:, lane_l:lane_l + 1]
    for hh in range(nh):
        o_ref[:, hh * KV_RANK:(hh + 1) * KV_RANK] = o[hh * tq:(hh + 1) * tq, :].astype(o_ref.dtype)


def _mla_prompt(qcat, kcat):
    nh, rows, dk = qcat.shape
    tq = ROW_TILE
    return pl.pallas_call(
        _mla_prompt_kernel,
        grid=(rows // tq,),
        in_specs=[pl.BlockSpec((nh, tq, dk), lambda i: (0, i, 0)),
                  pl.BlockSpec((rows, dk), lambda i: (0, 0), pipeline_mode=pl.Buffered(1))],
        out_specs=pl.BlockSpec((tq, nh * KV_RANK), lambda i: (i, 0)),
        out_shape=jax.ShapeDtypeStruct((rows, nh * KV_RANK), BF16),
        scratch_shapes=[pltpu.VMEM((nh * tq, LANES), F32), pltpu.VMEM((nh * tq, dk), F32)],
        compiler_params=_cparams(("arbitrary",)),
        name="mla_attn_prompt",
    )(qcat, kcat)


def _merge_kernel(oa_ref, ob_ref, ga_ref, gb_ref, x_ref, gt_ref, g_ref, wpa_ref, wuv_ref, wpb_ref,
                  wout_ref, x1_ref):
    y_a = _dot(oa_ref[0], wpa_ref[0])
    for hh in range(1, H_A):
        y_a = y_a + _dot(oa_ref[hh], wpa_ref[hh])
    o_b = _dot(ob_ref[...], wuv_ref[...])
    y_b = _dot(o_b.astype(BF16), wpb_ref[...])
    merged = jax.nn.sigmoid(ga_ref[...]) * y_a + jax.nn.sigmoid(gb_ref[...]) * y_b
    m = _dot(merged.astype(BF16), wout_ref[...])
    x1_ref[...] = x_ref[...] + gt_ref[...] * _rms(m, g_ref[...])


def _merge(oa, ob, ga, gb, x2, gt1, g_post, wpa_pad, wuv_bd, wpb, wout):
    rows = x2.shape[0]
    tm = ROW_TILE
    row = lambda w: pl.BlockSpec((tm, w), lambda i: (i, 0))
    const = lambda a: pl.BlockSpec(a.shape, lambda i: (0,) * a.ndim)
    mod = const(gt1) if gt1.shape[0] == 1 else row(D_MODEL)
    return pl.pallas_call(
        _merge_kernel,
        grid=(rows // tm,),
        in_specs=[pl.BlockSpec((H_A, tm, LANES), lambda i: (0, i, 0)), row(H_B * KV_RANK),
                  row(D_MODEL), row(D_MODEL), row(D_MODEL), mod, const(g_post), const(wpa_pad),
                  const(wuv_bd), const(wpb), const(wout)],
        out_specs=row(D_MODEL),
        out_shape=jax.ShapeDtypeStruct((rows, D_MODEL), F32),
        compiler_params=_cparams(("arbitrary",)),
        name="mixer_merge",
    )(oa, ob, ga, gb, x2, gt1, g_post, wpa_pad, wuv_bd, wpb, wout)


def _mlp_kernel(x1_ref, sc_ref, sh_ref, gt_ref, gpre_ref, gpost_ref, wup_ref, wdn_ref, y_ref):
    x1 = x1_ref[...]
    h2 = _rms(x1, gpre_ref[...]) * (1.0 + sc_ref[...]) + sh_ref[...]
    u = jnp.maximum(_dot(h2.astype(BF16), wup_ref[...]), 0.0)
    d = _dot((u * u).astype(BF16), wdn_ref[...])
    y_ref[...] = x1 + gt_ref[...] * _rms(d, gpost_ref[...])


def _mlp(x1, sc2, sh2, gt2, g_pre, g_post, wup, wdn):
    rows = x1.shape[0]
    tm = ROW_TILE
    row = lambda w: pl.BlockSpec((tm, w), lambda i: (i, 0))
    const = lambda a: pl.BlockSpec(a.shape, lambda i: (0,) * a.ndim)
    mod = (lambda a: const(a)) if sc2.shape[0] == 1 else (lambda a: row(D_MODEL))
    return pl.pallas_call(
        _mlp_kernel,
        grid=(rows // tm,),
        in_specs=[row(D_MODEL), mod(sc2), mod(sh2), mod(gt2), const(g_pre), const(g_post),
                  const(wup), const(wdn)],
        out_specs=row(D_MODEL),
        out_shape=jax.ShapeDtypeStruct((rows, D_MODEL), F32),
        compiler_params=_cparams(("arbitrary",)),
        name="sqrelu_mlp",
    )(x1, sc2, sh2, gt2, g_pre, g_post, wup, wdn)


RING_SLOTS = 3
RING_AHEAD = RING_SLOTS - 1


def _ring_step(step, total, copies):
    for k in range(RING_AHEAD):
        @pl.when(jnp.logical_and(step == 0, k < total))
        def _():
            for cp in copies(k, k):
                cp.start()

    @pl.when(step + RING_AHEAD < total)
    def _():
        for cp in copies(step + RING_AHEAD, (step + RING_AHEAD) % RING_SLOTS):
            cp.start()

    slot = step % RING_SLOTS
    for cp in copies(step, slot):
        cp.wait()
    return slot


def _select_sample_kernel(pc, pt_ref, q_ref, ck_ref, sel_ref, buf, kmt, sem):
    step = pl.program_id(0)
    n_pages = pt_ref.shape[1]
    nch = n_pages // pc
    nb = n_pages // 2

    def copies(st, slot):
        b, c = st // nch, st % nch
        return [pltpu.make_async_copy(ck_ref.at[pt_ref[b, c * pc + p]], buf.at[slot, p], sem.at[slot])
                for p in range(pc)]

    slot = _ring_step(step, pl.num_programs(0), copies)
    c = step % nch

    @pl.when(c == 0)
    def _():
        kmt[...] = jnp.zeros(kmt.shape, F32)

    lane = _lane_iota((DH_A, LANES))
    for hh in range(H_A):
        rows = slice((hh % 2) * DH_A, (hh % 2 + 1) * DH_A)
        acc = kmt[hh // 2, rows, :]
        for j in range(pc // 2):
            pair = buf[slot, 2 * j, hh] + buf[slot, 2 * j + 1, hh]
            col = jnp.sum(pair, axis=1, keepdims=True) * (1.0 / MOBA_BLOCK)
            acc = jnp.where(lane == c * (pc // 2) + j, col, acc)
        kmt[hh // 2, rows, :] = acc

    @pl.when(c == nch - 1)
    def _():
        q = q_ref[0]
        rows = q.shape[0]
        lane_q = _lane_iota((rows, LANES))
        out = jnp.zeros((rows, LANES), jnp.int32)
        for hh in range(H_A):
            sl = slice((hh // 2) * LANES, (hh // 2 + 1) * LANES)
            half_mask = (lane_q < DH_A) if hh % 2 == 0 else (lane_q >= DH_A)
            scores = jnp.dot(jnp.where(half_mask, q[:, sl], 0.0), kmt[hh // 2],
                             precision=lax.Precision.HIGHEST, preferred_element_type=F32)
            _, idxs = _top3_lanes(scores, lane_q < nb)
            for r, idx in enumerate(idxs):
                out = jnp.where(lane_q == hh * MOBA_TOPK + r, idx, out)
        sel_ref[0] = out


def _select_sample(page_table, q_pad, ck_t):
    db, n_pages = page_table.shape
    rows = q_pad.shape[1]
    pc = next(c for c in (16, 8, 4, 2) if n_pages % c == 0)
    nch = n_pages // pc
    return pl.pallas_call(
        functools.partial(_select_sample_kernel, pc),
        grid_spec=pltpu.PrefetchScalarGridSpec(
            num_scalar_prefetch=1, grid=(db * nch,),
            in_specs=[pl.BlockSpec((1, rows, 512), lambda s, pt: (s // nch, 0, 0)),
                      pl.BlockSpec(memory_space=pl.ANY)],
            out_specs=pl.BlockSpec((1, rows, LANES), lambda s, pt: (s // nch, 0, 0)),
            scratch_shapes=[pltpu.VMEM((RING_SLOTS, pc, H_A, DH_A, PAGE_SIZE), F32),
                            pltpu.VMEM((H_A // 2, 2 * DH_A, LANES), F32),
                            pltpu.SemaphoreType.DMA((RING_SLOTS,))]),
        out_shape=jax.ShapeDtypeStruct((db, rows, LANES), jnp.int32),
        compiler_params=_cparams(("arbitrary",)),
        name="moba_select_sample",
    )(page_table, q_pad, ck_t)


def _moba_sample_kernel(n_tok, pt_ref, sel_ref, q_ref, kn_ref, vn_ref, ck_ref, cv_ref, o_ref,
                        kbuf, vbuf, acc, sem):
    step = pl.program_id(0)
    t = step % n_tok

    def copies(st, slot):
        b, tt = st // n_tok, st % n_tok
        out = []
        for hh in range(H_A):
            for n in range(MOBA_TOPK):
                blk = sel_ref[b, tt * (H_A * MOBA_TOPK) + hh * MOBA_TOPK + n]
                for pg in range(2):
                    phys = pt_ref[b, blk * 2 + pg]
                    dst = pl.ds((n * 2 + pg) * PAGE_SIZE, PAGE_SIZE)
                    out.append(pltpu.make_async_copy(ck_ref.at[phys, hh], kbuf.at[slot, hh, :, dst],
                                                     sem.at[0, slot]))
                    out.append(pltpu.make_async_copy(cv_ref.at[phys, hh], vbuf.at[slot, hh, :, dst],
                                                     sem.at[1, slot]))
        return out

    slot = _ring_step(step, pl.num_programs(0), copies)

    @pl.when(t == 0)
    def _():
        acc[...] = jnp.zeros(acc.shape, F32)

    scale = DH_A ** -0.5
    q = q_ref[0]
    kn = kn_ref[0]
    vn = vn_ref[0]
    rows = q.shape[0]
    r_i = lax.broadcasted_iota(jnp.int32, (rows, rows), 0)
    c_i = lax.broadcasted_iota(jnp.int32, (rows, rows), 1)
    own_ok = jnp.logical_and(c_i <= r_i, c_i < n_tok)
    row_is_t = lax.broadcasted_iota(jnp.int32, (rows, DH_A), 0) == t
    for hh in range(H_A):
        sl = slice(hh * DH_A, (hh + 1) * DH_A)
        qh = (q[:, sl] * scale).astype(BF16)
        ks_t = kbuf[slot, hh].astype(BF16)
        vs_t = vbuf[slot, hh].astype(BF16)
        s = _dot(qh, ks_t)
        s_own = jnp.where(own_ok, _dot_nt(qh, kn[:, sl].astype(BF16)), NEG_INF)
        m = jnp.maximum(jnp.max(s, axis=1, keepdims=True), jnp.max(s_own, axis=1, keepdims=True))
        p = jnp.exp(s - m)
        p_own = jnp.exp(s_own - m)
        l = jnp.sum(p, axis=1, keepdims=True) + jnp.sum(p_own, axis=1, keepdims=True)
        o = (_dot_nt(p.astype(BF16), vs_t) + _dot(p_own.astype(BF16), vn[:, sl].astype(BF16))) / l
        acc[:, sl] = jnp.where(row_is_t, o, acc[:, sl])
    o_ref[0] = acc[...]


def _moba_sample(page_table, sel, q_pad, kn_pad, vn_pad, ck_t, cv_t, n_tok):
    db, rows, _ = q_pad.shape
    n_keys = MOBA_TOPK * 2 * PAGE_SIZE
    tok = lambda: pl.BlockSpec((1, rows, 512), lambda s, pt, sl: (s // n_tok, 0, 0))
    return pl.pallas_call(
        functools.partial(_moba_sample_kernel, n_tok),
        grid_spec=pltpu.PrefetchScalarGridSpec(
            num_scalar_prefetch=2, grid=(db * n_tok,),
            in_specs=[tok(), tok(), tok(), pl.BlockSpec(memory_space=pl.ANY),
                      pl.BlockSpec(memory_space=pl.ANY)],
            out_specs=tok(),
            scratch_shapes=[pltpu.VMEM((RING_SLOTS, H_A, DH_A, n_keys), F32),
                            pltpu.VMEM((RING_SLOTS, H_A, DH_A, n_keys), F32),
                            pltpu.VMEM((rows, 512), F32),
                            pltpu.SemaphoreType.DMA((2, RING_SLOTS))]),
        out_shape=jax.ShapeDtypeStruct((db, rows, 512), F32),
        compiler_params=_cparams(("arbitrary",)),
        name="moba_attn_sample",
    )(page_table, sel, q_pad, kn_pad, vn_pad, ck_t, cv_t)


def _mla_sample_kernel(n_tok, pages, pt_ref, q_ref, kn_ref, cl_ref, cr_ref, o_ref,
                       lbuf, rbuf, m_sc, l_sc, acc_sc, sem):
    step = pl.program_id(0)
    nc = pt_ref.shape[1] // pages
    c = step % nc

    def copies(st, slot):
        b, cc = st // nc, st % nc
        out = []
        for p in range(pages):
            phys = pt_ref[b, cc * pages + p]
            dst = pl.ds(p * PAGE_SIZE, PAGE_SIZE)
            out.append(pltpu.make_async_copy(cl_ref.at[phys], lbuf.at[slot, dst, :], sem.at[0, slot]))
            out.append(pltpu.make_async_copy(cr_ref.at[phys], rbuf.at[slot, :, dst], sem.at[1, slot]))
        return out

    slot = _ring_step(step, pl.num_programs(0), copies)

    @pl.when(c == 0)
    def _():
        m_sc[...] = jnp.full(m_sc.shape, -jnp.inf, F32)
        l_sc[...] = jnp.zeros(l_sc.shape, F32)
        acc_sc[...] = jnp.zeros(acc_sc.shape, F32)

    q = q_ref[0]
    q_lat, q_rope = q[:, :KV_RANK], q[:, KV_RANK:KV_RANK + ROPE_B]

    def update(s, v):
        m_prev = m_sc[...]
        m_next = jnp.maximum(m_prev, jnp.max(s, axis=1, keepdims=True))
        alpha = jnp.exp(m_prev - m_next)
        p = jnp.exp(s - m_next[:, :1])
        l_sc[...] = l_sc[...] * alpha + jnp.sum(p, axis=1, keepdims=True)
        acc_sc[...] = acc_sc[...] * alpha + _dot(p.astype(BF16), v)
        m_sc[...] = m_next

    lat = lbuf[slot].astype(BF16)
    s = _dot_nt(q_lat, lat) + _dot(q_rope, rbuf[slot].astype(BF16))
    update(s, lat)

    @pl.when(c == nc - 1)
    def _():
        kn = kn_ref[0]
        rows, n_new = q.shape[0], kn.shape[0]
        s_new = _dot_nt(q_lat, kn[:, :KV_RANK]) + _dot_nt(q_rope, kn[:, KV_RANK:KV_RANK + ROPE_B])
        tok = lax.broadcasted_iota(jnp.int32, (rows, n_new), 0) // H_B
        col = lax.broadcasted_iota(jnp.int32, (rows, n_new), 1)
        s_new = jnp.where(jnp.logical_and(col <= tok, col < n_tok), s_new, NEG_INF)
        update(s_new, kn[:, :KV_RANK])
        o_ref[0] = (acc_sc[...] / l_sc[...]).astype(o_ref.dtype)


def _mla_sample(page_table, q_rows, kn_pad, cache_latent, ckr_t, n_tok):
    db, rows, dk = q_rows.shape
    n_pages = page_table.shape[1]
    pages = next(c for c in (32, 16, 8, 4, 2, 1) if n_pages % c == 0)
    nc = n_pages // pages
    return pl.pallas_call(
        functools.partial(_mla_sample_kernel, n_tok, pages),
        grid_spec=pltpu.PrefetchScalarGridSpec(
            num_scalar_prefetch=1, grid=(db * nc,),
            in_specs=[pl.BlockSpec((1, rows, dk), lambda s, pt: (s // nc, 0, 0)),
                      pl.BlockSpec((1,) + kn_pad.shape[1:], lambda s, pt: (s // nc, 0, 0)),
                      pl.BlockSpec(memory_space=pl.ANY), pl.BlockSpec(memory_space=pl.ANY)],
            out_specs=pl.BlockSpec((1, rows, KV_RANK), lambda s, pt: (s // nc, 0, 0)),
            scratch_shapes=[pltpu.VMEM((RING_SLOTS, pages * PAGE_SIZE, KV_RANK), F32),
                            pltpu.VMEM((RING_SLOTS, ROPE_B, pages * PAGE_SIZE), F32),
                            pltpu.VMEM((rows, LANES), F32), pltpu.VMEM((rows, LANES), F32),
                            pltpu.VMEM((rows, KV_RANK), F32),
                            pltpu.SemaphoreType.DMA((2, RING_SLOTS))]),
        out_shape=jax.ShapeDtypeStruct((db, rows, KV_RANK), BF16),
        compiler_params=_cparams(("arbitrary",)),
        name="mla_attn_sample",
    )(page_table, q_rows, kn_pad, cache_latent, ckr_t)


def _prepare_weights(w_in, w_uq, w_uk, w_uv, w_proj_a, w_proj_b, w_out, w_up, w_down):
    o = 3 * H_A * DH_A
    segs = [w_in[:, :o + Q_RANK + KV_RANK], w_in[:, o + Q_RANK + KV_RANK + ROPE_B:],
            w_in[:, o + Q_RANK + KV_RANK:o + Q_RANK + KV_RANK + ROPE_B],
            jnp.zeros((D_MODEL, LANES - ROPE_B), w_in.dtype)]
    w_in_p = jnp.concatenate(segs, axis=1).astype(BF16)
    wq = w_uq.reshape(Q_RANK, H_B, NOPE_B + ROPE_B)
    wuq_n = wq[:, :, :NOPE_B].reshape(Q_RANK, H_B * NOPE_B).astype(BF16)
    wuq_r = jnp.pad(wq[:, :, NOPE_B:], ((0, 0), (0, 0), (0, LANES - ROPE_B)))
    wuq_r = wuq_r.reshape(Q_RANK, H_B * LANES).astype(BF16)
    eye = jnp.eye(H_B, dtype=w_uk.dtype)
    wuk_bd = jnp.einsum('rhn,hg->hngr', w_uk, eye).reshape(H_B * NOPE_B, H_B * KV_RANK).astype(BF16)
    wuv_bd = jnp.einsum('rhv,hg->hrgv', w_uv, eye).reshape(H_B * KV_RANK, H_B * V_B).astype(BF16)
    wpa_pad = jnp.pad(w_proj_a.reshape(H_A, DH_A, D_MODEL), ((0, 0), (0, LANES - DH_A), (0, 0)))
    return dict(w_in_p=w_in_p, wuq_n=wuq_n, wuq_r=wuq_r, wuk_bd=wuk_bd, wuv_bd=wuv_bd,
                wpa_pad=wpa_pad.astype(BF16), wpb=w_proj_b.astype(BF16), wout=w_out.astype(BF16),
                wup=w_up.astype(BF16), wdn=w_down.astype(BF16))


def _row(g):
    return g.reshape(1, -1)


def _trunk(x2, mods, pos, attend, prompt, wts, g_pre_mix, g_post_mix, g_pre_mlp, g_post_mlp,
           g_q_lat, g_kv_lat):
    sh1, sc1, gt1, sh2, sc2, gt2 = mods
    tabs_a = _rope_tables(pos, ROT_A, THETA_A, DH_A)
    tabs_b = _rope_tables(pos, ROPE_B, THETA_B, ROPE_B)
    outs = _proj_in(x2, sc1, sh1, _row(g_pre_mix), wts['w_in_p'], _row(g_q_lat), _row(g_kv_lat),
                    wts['wuq_n'], wts['wuq_r'], wts['wuk_bd'], tabs_a, tabs_b, prompt)
    qa, k, v, latent, krope, ga, gb, qcat, kcat = outs[:9]
    oa, ob = attend(qa, k, v, qcat, kcat, outs[9:])
    x1 = _merge(oa, ob, ga, gb, x2, gt1, _row(g_post_mix), wts['wpa_pad'], wts['wuv_bd'],
                wts['wpb'], wts['wout'])
    y = _mlp(x1, sc2, sh2, gt2, _row(g_pre_mlp), _row(g_post_mlp), wts['wup'], wts['wdn'])
    return y, k, v, latent, krope


def kernel(x_prompt, x_sample, c_prompt, c_sample, cache_k, cache_v, cache_latent, cache_krope, page_table, w_ada, b_ada, g_pre_mix, g_post_mix, g_pre_mlp, g_post_mlp, w_in, g_q_lat, g_kv_lat, w_uq, w_uk, w_uv, w_proj_a, w_proj_b, w_out, w_up, w_down):
    bsz, seq, _ = x_prompt.shape
    db, n_tok, _ = x_sample.shape
    n_pages = page_table.shape[1]
    past = n_pages * PAGE_SIZE
    assert bsz == 1 and seq % ROW_TILE == 0 and seq // MOBA_BLOCK <= N_SEL
    assert (db * n_tok) % ROW_TILE == 0 and n_tok <= 8
    assert past % MOBA_BLOCK == 0 and MOBA_TOPK <= past // MOBA_BLOCK <= LANES

    wts = _prepare_weights(w_in, w_uq, w_uk, w_uv, w_proj_a, w_proj_b, w_out, w_up, w_down)
    gains = (g_pre_mix, g_post_mix, g_pre_mlp, g_post_mlp, g_q_lat, g_kv_lat)

    n_c = bsz + db
    c_all = jnp.pad(jnp.concatenate([c_prompt, c_sample], axis=0), ((0, -n_c % 8), (0, 0)))
    mod_all = _modulation(c_all, w_ada, b_ada)
    mods_p = tuple(jnp.split(mod_all[:bsz], 6, axis=1))
    mods_s = tuple(jnp.split(jnp.repeat(mod_all[bsz:n_c], n_tok, axis=0), 6, axis=1))

    def attend_prompt(qa, k, v, qcat, kcat, extra):
        kaug, vaug, kmean = extra
        nb = kmean.shape[0]
        kmean_ext = jnp.pad(kmean.reshape(nb, 512), ((N_SEL, LANES - N_SEL - nb), (0, 0)))
        qaug = _select_prompt(qa, kmean_ext)
        return _moba_prompt(qaug, kaug, vaug), _mla_prompt(qcat, kcat)

    pos_p = jnp.arange(seq, dtype=jnp.int32)
    y_p, k_p, v_p, lat_p, kr_p = _trunk(x_prompt.reshape(seq, D_MODEL), mods_p, pos_p, attend_prompt,
                                        True, wts, *gains)

    rows_pad = 8
    ck_t = cache_k.transpose(0, 2, 3, 1)
    cv_t = cache_v.transpose(0, 2, 3, 1)
    ckr_t = cache_krope.transpose(0, 2, 1)

    def attend_sample(qa, k, v, qcat, kcat, extra):
        pad_tok = lambda a: jnp.pad(a.reshape(db, n_tok, a.shape[-1]),
                                    ((0, 0), (0, rows_pad - n_tok), (0, 0)))
        q_pad, kn_pad, vn_pad = pad_tok(qa), pad_tok(k), pad_tok(v)
        sel = _select_sample(page_table, q_pad, ck_t)
        sel = sel[:, :n_tok, :H_A * MOBA_TOPK].reshape(db, n_tok * H_A * MOBA_TOPK)
        oa = _moba_sample(page_table, sel, q_pad, kn_pad, vn_pad, ck_t, cv_t, n_tok)
        oa = oa[:, :n_tok].reshape(db * n_tok, H_A, DH_A).transpose(1, 0, 2)
        oa = jnp.pad(oa, ((0, 0), (0, 0), (0, LANES - DH_A))).astype(BF16)
        q_rows = qcat.reshape(H_B, db, n_tok, 2 * LANES).transpose(1, 2, 0, 3)
        q_rows = q_rows.reshape(db, n_tok * H_B, 2 * LANES)
        ob = _mla_sample(page_table, q_rows, pad_tok(kcat), cache_latent, ckr_t, n_tok)
        return oa, ob.reshape(db * n_tok, H_B * KV_RANK)

    pos_s = past + jnp.tile(jnp.arange(n_tok, dtype=jnp.int32), db)
    y_s, k_s, v_s, lat_s, kr_s = _trunk(x_sample.reshape(db * n_tok, D_MODEL), mods_s, pos_s,
                                        attend_sample, False, wts, *gains)

    return (y_p.reshape(bsz, seq, D_MODEL), y_s.reshape(db, n_tok, D_MODEL),
            k_p.reshape(bsz, seq, H_A, DH_A), v_p.reshape(bsz, seq, H_A, DH_A),
            lat_p.reshape(bsz, seq, KV_RANK), kr_p.reshape(bsz, seq, ROPE_B),
            k_s.reshape(db, n_tok, H_A, DH_A), v_s.reshape(db, n_tok, H_A, DH_A),
            lat_s.reshape(db, n_tok, KV_RANK), kr_s.reshape(db, n_tok, ROPE_B))
```

```python
import functools

import jax
import jax.numpy as jnp
from jax import lax
from jax.experimental import pallas as pl
from jax.experimental.pallas import tpu as pltpu

F32 = jnp.float32
BF16 = jnp.bfloat16

D_MODEL = 1024
H_A = 8
DH_A = 64
ROT_A = DH_A // 4
THETA_A = 500000.0
MOBA_BLOCK = 256
MOBA_TOPK = 3
H_B = 8
Q_RANK = 256
KV_RANK = 128
NOPE_B = 64
ROPE_B = 32
V_B = 64
THETA_B = 10000.0
D_FF = 4 * D_MODEL
PAGE_SIZE = 128
EPS = 1e-6
NEG_INF = -1e30
LOG2E = 1.4426950408889634

LANES = 128
ROW_TILE = 256
N_SEL = 64
VMEM_LIMIT = 56 * 1024 * 1024

C_QA, C_KA, C_VA = 0, 512, 1024
C_QLAT, C_KVLAT = 1536, 1792
C_GA, C_GB, C_KR = 1920, 2944, 3968
D_IN_PAD = 4096
ONES_LANE_A = DH_A
ONES_LANE_B = ROPE_B


def _cparams(sem, vmem=VMEM_LIMIT):
    return pltpu.CompilerParams(dimension_semantics=sem, vmem_limit_bytes=vmem)


def _rms(x, g):
    return x * lax.rsqrt(jnp.mean(x * x, axis=-1, keepdims=True) + EPS) * g


def _dot(a, b):
    return jnp.dot(a, b, preferred_element_type=F32)


def _dot_nt(a, b, precision=None):
    return lax.dot_general(a, b, (((1,), (1,)), ((), ())), precision=precision,
                           preferred_element_type=F32)


def _lane_iota(shape):
    return lax.broadcasted_iota(jnp.int32, shape, len(shape) - 1)


def _rope_slab(x, c, s_up, s_dn, half):
    return x * c + pltpu.roll(x, half, 1) * s_up + pltpu.roll(x, LANES - half, 1) * s_dn


def _rope_tables(pos, rot, theta, group):
    half = rot // 2
    inv = 1.0 / (theta ** (jnp.arange(half, dtype=F32) / half))
    ang = pos.astype(F32)[:, None] * inv[None, :]
    cos, sin = jnp.cos(ang), jnp.sin(ang)
    n = pos.shape[0]
    z_half = jnp.zeros((n, half), F32)
    rest1 = jnp.ones((n, group - rot), F32)
    rest0 = jnp.zeros((n, group - rot), F32)
    c = jnp.concatenate([cos, cos, rest1], axis=1)
    s_up = jnp.concatenate([z_half, sin, rest0], axis=1)
    s_dn = jnp.concatenate([-sin, z_half, rest0], axis=1)
    rep = LANES // group
    return tuple(jnp.tile(t, (1, rep)) for t in (c, s_up, s_dn))


def _mod_kernel(c_ref, w_ref, b_ref, o_ref):
    c = c_ref[...]
    a = (c * jax.nn.sigmoid(c)).astype(BF16)
    o_ref[...] = _dot(a, w_ref[...].astype(BF16)) + b_ref[...]


def _modulation(c_all, w_ada, b_ada):
    rows = c_all.shape[0]
    n = w_ada.shape[1]
    tn = 768
    return pl.pallas_call(
        _mod_kernel,
        grid=(n // tn,),
        in_specs=[pl.BlockSpec((rows, D_MODEL), lambda j: (0, 0)),
                  pl.BlockSpec((D_MODEL, tn), lambda j: (0, j)),
                  pl.BlockSpec((1, tn), lambda j: (0, j))],
        out_specs=pl.BlockSpec((rows, tn), lambda j: (0, j)),
        out_shape=jax.ShapeDtypeStruct((rows, n), F32),
        compiler_params=_cparams(("arbitrary",)),
        name="adaln_modulation",
    )(c_all, w_ada, b_ada.reshape(1, n))


def _proj_in_kernel(prompt, x_ref, sc_ref, sh_ref, g_ref, w_ref, gq_ref, gkv_ref, wuqn_ref, wuqr_ref,
                    wuk_ref, ca_ref, sau_ref, sad_ref, cb_ref, sbu_ref, sbd_ref,
                    qa_ref, k_ref, v_ref, lat_ref, kr_ref, ga_ref, gb_ref, qcat_ref, kcat_ref,
                    *moba_refs):
    tm = x_ref.shape[0]
    x = x_ref[...]
    h = _rms(x, g_ref[...]) * (1.0 + sc_ref[...]) + sh_ref[...]
    p = _dot(h.astype(BF16), w_ref[...])

    ca, sau, sad = ca_ref[...], sau_ref[...], sad_ref[...]
    cb, sbu, sbd = cb_ref[...], sbu_ref[...], sbd_ref[...]

    def rope_a(y):
        return jnp.concatenate(
            [_rope_slab(y[:, s * LANES:(s + 1) * LANES], ca, sau, sad, ROT_A // 2)
             for s in range(H_A * DH_A // LANES)], axis=1)

    qa = rope_a(p[:, C_QA:C_QA + 512])
    ka = rope_a(p[:, C_KA:C_KA + 512])
    va = p[:, C_VA:C_VA + 512]
    qa_ref[...] = qa
    k_ref[...] = ka
    v_ref[...] = va
    ga_ref[...] = p[:, C_GA:C_GA + D_MODEL]
    gb_ref[...] = p[:, C_GB:C_GB + D_MODEL]

    latent = _rms(p[:, C_KVLAT:C_KVLAT + KV_RANK], gkv_ref[...])
    lat_ref[...] = latent
    kr = _rope_slab(p[:, C_KR:C_KR + LANES], cb, sbu, sbd, ROPE_B // 2)
    kr_ref[...] = kr[:, :ROPE_B]
    lane = _lane_iota((tm, LANES))
    kr_slab = jnp.where(lane == ONES_LANE_B, 1.0, kr)
    kcat_ref[...] = jnp.concatenate([latent, kr_slab], axis=1).astype(BF16)

    scale_b = (NOPE_B + ROPE_B) ** -0.5 * LOG2E
    qn = _rms(p[:, C_QLAT:C_QLAT + Q_RANK], gq_ref[...]).astype(BF16)
    q_nope = _dot(qn, wuqn_ref[...])
    q_rope = _dot(qn, wuqr_ref[...])
    q_abs = _dot(q_nope.astype(BF16), wuk_ref[...])
    for hh in range(H_B):
        sl = slice(hh * LANES, (hh + 1) * LANES)
        qr = _rope_slab(q_rope[:, sl], cb, sbu, sbd, ROPE_B // 2)
        qcat_ref[hh] = (jnp.concatenate([q_abs[:, sl], qr], axis=1) * scale_b).astype(BF16)

    if prompt:
        kaug_ref, vaug_ref, kmean_ref = moba_refs
        blk = pl.program_id(0)
        onehot = jnp.where(lane == N_SEL + blk, 1.0, 0.0)
        ones_col = jnp.where(lane == ONES_LANE_A, 1.0, 0.0)
        for hh in range(H_A):
            sl = slice((hh // 2) * LANES, (hh // 2 + 1) * LANES)
            ks, vs = ka[:, sl], va[:, sl]
            if hh % 2 == 1:
                ks = pltpu.roll(ks, DH_A, 1)
                vs = pltpu.roll(vs, DH_A, 1)
            kaug_ref[hh] = jnp.where(lane < DH_A, ks, onehot).astype(BF16)
            vaug_ref[hh] = jnp.where(lane < DH_A, vs, ones_col).astype(BF16)
        kmean_ref[0] = jnp.sum(ka, axis=0, keepdims=True) * (1.0 / MOBA_BLOCK)


def _proj_in(x2, sc, sh, g_pre, w_in_p, g_q, g_kv, wuq_n, wuq_r, wuk_bd, tabs_a, tabs_b, prompt):
    rows = x2.shape[0]
    tm = ROW_TILE
    nt = rows // tm
    row = lambda w: pl.BlockSpec((tm, w), lambda i: (i, 0))
    const = lambda a: pl.BlockSpec(a.shape, lambda i: (0,) * a.ndim)
    mod = (lambda a: const(a)) if sc.shape[0] == 1 else (lambda a: row(a.shape[1]))
    heads = lambda w: pl.BlockSpec((H_A, tm, w), lambda i: (0, i, 0))
    in_specs = [row(D_MODEL), mod(sc), mod(sh), const(g_pre), const(w_in_p), const(g_q), const(g_kv),
                const(wuq_n), const(wuq_r), const(wuk_bd)] + [row(LANES)] * 6
    out_shape = [jax.ShapeDtypeStruct((rows, 512), F32)] * 3 + [
        jax.ShapeDtypeStruct((rows, KV_RANK), F32), jax.ShapeDtypeStruct((rows, ROPE_B), F32),
        jax.ShapeDtypeStruct((rows, D_MODEL), F32), jax.ShapeDtypeStruct((rows, D_MODEL), F32),
        jax.ShapeDtypeStruct((H_B, rows, 2 * LANES), BF16), jax.ShapeDtypeStruct((rows, 2 * LANES), BF16)]
    out_specs = [row(512)] * 3 + [row(KV_RANK), row(ROPE_B), row(D_MODEL), row(D_MODEL),
                                  heads(2 * LANES), row(2 * LANES)]
    if prompt:
        out_shape += [jax.ShapeDtypeStruct((H_A, rows, LANES), BF16)] * 2 + [
            jax.ShapeDtypeStruct((nt, 1, 512), F32)]
        out_specs += [heads(LANES), heads(LANES), pl.BlockSpec((1, 1, 512), lambda i: (i, 0, 0))]
    return pl.pallas_call(
        functools.partial(_proj_in_kernel, prompt),
        grid=(nt,), in_specs=in_specs, out_specs=out_specs, out_shape=out_shape,
        compiler_params=_cparams(("arbitrary",)),
        name="proj_in_prompt" if prompt else "proj_in_sample",
    )(x2, sc, sh, g_pre, w_in_p, g_q, g_kv, wuq_n, wuq_r, wuk_bd, *tabs_a, *tabs_b)


def _top3_lanes(scores, valid):
    lane = _lane_iota(scores.shape)
    s = jnp.where(valid, scores, NEG_INF)
    picked = jnp.zeros(scores.shape, jnp.bool_)
    idxs = []
    for _ in range(MOBA_TOPK):
        mx = jnp.max(s, axis=1, keepdims=True)
        idx = jnp.min(jnp.where(s == mx, lane, 2 * LANES), axis=1, keepdims=True)
        hit = lane == idx
        picked = jnp.logical_or(picked, hit)
        s = jnp.where(hit, -jnp.inf, s)
        idxs.append(idx)
    return picked, idxs


def _select_prompt_kernel(qa_ref, kmean_ref, qaug_ref):
    tm = qa_ref.shape[0]
    own = pl.program_id(0)
    lane = _lane_iota((tm, LANES))
    scale = DH_A ** -0.5 * LOG2E
    scores = []
    for hh in range(H_A):
        sl = slice((hh // 2) * LANES, (hh // 2 + 1) * LANES)
        half_mask = (lane < DH_A) if hh % 2 == 0 else (lane >= DH_A)
        scores.append(_dot_nt(jnp.where(half_mask, qa_ref[:, sl], 0.0), kmean_ref[:, sl],
                              precision=lax.Precision.HIGHEST))
    scores = jnp.concatenate(scores, axis=0)
    blk = _lane_iota(scores.shape) - N_SEL
    valid = jnp.logical_and(blk >= 0, blk < own)
    picked, _ = _top3_lanes(scores, valid)
    keep = jnp.logical_or(jnp.logical_and(picked, valid), blk == own)
    bias = jnp.where(keep, 0.0, NEG_INF)
    for hh in range(H_A):
        qs = qa_ref[:, (hh // 2) * LANES:(hh // 2 + 1) * LANES]
        qh = qs if hh % 2 == 0 else pltpu.roll(qs, DH_A, 1)
        qaug_ref[hh] = jnp.where(lane < DH_A, qh * scale, bias[hh * tm:(hh + 1) * tm]).astype(BF16)


def _select_prompt(qa, kmean_ext):
    rows = qa.shape[0]
    tm = ROW_TILE
    return pl.pallas_call(
        _select_prompt_kernel,
        grid=(rows // tm,),
        in_specs=[pl.BlockSpec((tm, 512), lambda i: (i, 0)),
                  pl.BlockSpec(kmean_ext.shape, lambda i: (0, 0))],
        out_specs=pl.BlockSpec((H_A, tm, LANES), lambda i: (0, i, 0)),
        out_shape=jax.ShapeDtypeStruct((H_A, rows, LANES), BF16),
        compiler_params=_cparams(("arbitrary",)),
        name="moba_select_prompt",
    )(qa, kmean_ext)


def _softmax_step(s, v, m_sc, acc_sc):
    reps = s.shape[1] // LANES
    m_prev = m_sc[...]
    m_next = jnp.maximum(m_prev, jnp.max(s, axis=1, keepdims=True))
    alpha = jnp.exp2(m_prev - m_next)
    p = jnp.exp2(s - jnp.tile(m_next, (1, reps)))
    acc = acc_sc[...]
    acc_sc[...] = acc * jnp.tile(alpha, (1, acc.shape[1] // LANES)) + _dot(p.astype(BF16), v)
    m_sc[...] = m_next


def _moba_prompt_kernel(q_ref, k_ref, v_ref, o_ref, sa_sc, sb_sc, m_sc, acc_sc):
    hb = q_ref.shape[0]
    own = pl.program_id(1)
    m_sc[...] = jnp.full(m_sc.shape, -jnp.inf, F32)
    acc_sc[...] = jnp.zeros(acc_sc.shape, F32)

    def rows_of(ref, hh, blk):
        return ref[hh, pl.ds(pl.multiple_of(blk * MOBA_BLOCK, MOBA_BLOCK), MOBA_BLOCK), :]

    def scores(hh, blk):
        return _dot_nt(q_ref[hh], rows_of(k_ref, hh, blk))

    def consume(cur, blk):
        for hh in range(hb):
            _softmax_step(cur[hh], rows_of(v_ref, hh, blk), m_sc.at[hh], acc_sc.at[hh])

    def stage(cur, nxt, blk_next, blk_cur):
        for hh in range(hb):
            nxt[hh] = scores(hh, blk_next)
        consume(cur, blk_cur)

    for hh in range(hb):
        s = scores(hh, own)
        r = lax.broadcasted_iota(jnp.int32, s.shape, 0)
        c = lax.broadcasted_iota(jnp.int32, s.shape, 1)
        sa_sc[hh] = jnp.where(c <= r, s, NEG_INF)

    def body(jj, carry):
        j0 = 2 * jj
        stage(sa_sc, sb_sc, j0, jnp.where(jj == 0, own, j0 - 1))
        stage(sb_sc, sa_sc, j0 + 1, j0)
        return carry

    lax.fori_loop(0, own // 2, body, 0)

    @pl.when(own % 2 == 1)
    def _():
        stage(sa_sc, sb_sc, own - 1, jnp.where(own == 1, own, own - 2))
        consume(sb_sc, own - 1)

    @pl.when(own % 2 == 0)
    def _():
        consume(sa_sc, jnp.maximum(own - 1, 0))

    lane = _lane_iota(acc_sc.shape[1:])
    for pr in range(hb // 2):
        a0, a1 = acc_sc[2 * pr], acc_sc[2 * pr + 1]
        o0 = a0 / a0[:, ONES_LANE_A:ONES_LANE_A + 1]
        o1 = a1 / a1[:, ONES_LANE_A:ONES_LANE_A + 1]
        o_ref[pr] = jnp.where(lane < DH_A, o0, pltpu.roll(o1, DH_A, 1)).astype(o_ref.dtype)


MOBA_HEADS_PER_STEP = 4


def _moba_prompt(qaug, kaug, vaug):
    _, rows, _ = qaug.shape
    tq = MOBA_BLOCK
    hb = MOBA_HEADS_PER_STEP
    resident = lambda: pl.BlockSpec((hb, rows, LANES), lambda g, i: (g, 0, 0),
                                    pipeline_mode=pl.Buffered(1))
    return pl.pallas_call(
        _moba_prompt_kernel,
        grid=(H_A // hb, rows // tq),
        in_specs=[pl.BlockSpec((hb, tq, LANES), lambda g, i: (g, i, 0)), resident(), resident()],
        out_specs=pl.BlockSpec((hb // 2, tq, LANES), lambda g, i: (g, i, 0)),
        out_shape=jax.ShapeDtypeStruct((H_A // 2, rows, LANES), BF16),
        scratch_shapes=[pltpu.VMEM((hb, tq, MOBA_BLOCK), F32), pltpu.VMEM((hb, tq, MOBA_BLOCK), F32),
                        pltpu.VMEM((hb, tq, LANES), F32), pltpu.VMEM((hb, tq, LANES), F32)],
        compiler_params=_cparams(("arbitrary", "arbitrary")),
        name="moba_attn_prompt",
    )(qaug, kaug, vaug)


def _mla_prompt_kernel(q_ref, k_ref, o_ref, m_sc, acc_sc):
    nh, tq, dk = q_ref.shape
    i = pl.program_id(0)
    q = q_ref[...].reshape(nh * tq, dk)
    m_sc[...] = jnp.full(m_sc.shape, -jnp.inf, F32)
    acc_sc[...] = jnp.zeros(acc_sc.shape, F32)

    def tile(j, causal):
        start = pl.multiple_of(j * tq, tq)
        k = k_ref[pl.ds(start, tq), :]
        s = _dot_nt(q, k)
        if causal:
            s3 = s.reshape(nh, tq, tq)
            r = lax.broadcasted_iota(jnp.int32, s3.shape, 1)
            c = lax.broadcasted_iota(jnp.int32, s3.shape, 2)
            s = jnp.where(c <= r, s3, NEG_INF).reshape(nh * tq, tq)
        _softmax_step(s, k, m_sc, acc_sc)

    tile(i, True)

    def body(j, carry):
        tile(j, False)
        return carry

    lax.fori_loop(0, i, body, 0)
    acc = acc_sc[...]
    lane_l = LANES + ONES_LANE_B
    o = acc[:, :KV_RANK] / acc[:, lane_l:lane_l + 1]
    for hh in range(nh):
        o_ref[:, hh * KV_RANK:(hh + 1) * KV_RANK] = o[hh * tq:(hh + 1) * tq, :].astype(o_ref.dtype)


def _mla_prompt(qcat, kcat):
    nh, rows, dk = qcat.shape
    tq = ROW_TILE
    return pl.pallas_call(
        _mla_prompt_kernel,
        grid=(rows // tq,),
        in_specs=[pl.BlockSpec((nh, tq, dk), lambda i: (0, i, 0)),
                  pl.BlockSpec((rows, dk), lambda i: (0, 0), pipeline_mode=pl.Buffered(1))],
        out_specs=pl.BlockSpec((tq, nh * KV_RANK), lambda i: (i, 0)),
        out_shape=jax.ShapeDtypeStruct((rows, nh * KV_RANK), BF16),
        scratch_shapes=[pltpu.VMEM((nh * tq, LANES), F32), pltpu.VMEM((nh * tq, dk), F32)],
        compiler_params=_cparams(("arbitrary",)),
        name="mla_attn_prompt",
    )(qcat, kcat)


def _merge_kernel(oa_ref, ob_ref, ga_ref, gb_ref, x_ref, gt_ref, g_ref, wpa_ref, wuv_ref, wpb_ref,
                  wout_ref, x1_ref):
    o_a = jnp.concatenate([oa_ref[pr] for pr in range(H_A // 2)], axis=1)
    y_a = _dot(o_a, wpa_ref[...])
    o_b = _dot(ob_ref[...], wuv_ref[...])
    y_b = _dot(o_b.astype(BF16), wpb_ref[...])
    merged = jax.nn.sigmoid(ga_ref[...]) * y_a + jax.nn.sigmoid(gb_ref[...]) * y_b
    m = _dot(merged.astype(BF16), wout_ref[...])
    x1_ref[...] = x_ref[...] + gt_ref[...] * _rms(m, g_ref[...])


def _merge(oa, ob, ga, gb, x2, gt1, g_post, wpa, wuv_bd, wpb, wout):
    rows = x2.shape[0]
    tm = ROW_TILE
    row = lambda w: pl.BlockSpec((tm, w), lambda i: (i, 0))
    const = lambda a: pl.BlockSpec(a.shape, lambda i: (0,) * a.ndim)
    mod = const(gt1) if gt1.shape[0] == 1 else row(D_MODEL)
    return pl.pallas_call(
        _merge_kernel,
        grid=(rows // tm,),
        in_specs=[pl.BlockSpec((H_A // 2, tm, LANES), lambda i: (0, i, 0)), row(H_B * KV_RANK),
                  row(D_MODEL), row(D_MODEL), row(D_MODEL), mod, const(g_post), const(wpa),
                  const(wuv_bd), const(wpb), const(wout)],
        out_specs=row(D_MODEL),
        out_shape=jax.ShapeDtypeStruct((rows, D_MODEL), F32),
        compiler_params=_cparams(("arbitrary",)),
        name="mixer_merge",
    )(oa, ob, ga, gb, x2, gt1, g_post, wpa, wuv_bd, wpb, wout)


def _mlp_kernel(x1_ref, sc_ref, sh_ref, gt_ref, gpre_ref, gpost_ref, wup_ref, wdn_ref, y_ref):
    x1 = x1_ref[...]
    h2 = _rms(x1, gpre_ref[...]) * (1.0 + sc_ref[...]) + sh_ref[...]
    u = jnp.maximum(_dot(h2.astype(BF16), wup_ref[...]), 0.0)
    d = _dot((u * u).astype(BF16), wdn_ref[...])
    y_ref[...] = x1 + gt_ref[...] * _rms(d, gpost_ref[...])


def _mlp(x1, sc2, sh2, gt2, g_pre, g_post, wup, wdn):
    rows = x1.shape[0]
    tm = ROW_TILE
    row = lambda w: pl.BlockSpec((tm, w), lambda i: (i, 0))
    const = lambda a: pl.BlockSpec(a.shape, lambda i: (0,) * a.ndim)
    mod = (lambda a: const(a)) if sc2.shape[0] == 1 else (lambda a: row(D_MODEL))
    return pl.pallas_call(
        _mlp_kernel,
        grid=(rows // tm,),
        in_specs=[row(D_MODEL), mod(sc2), mod(sh2), mod(gt2), const(g_pre), const(g_post),
                  const(wup), const(wdn)],
        out_specs=row(D_MODEL),
        out_shape=jax.ShapeDtypeStruct((rows, D_MODEL), F32),
        compiler_params=_cparams(("arbitrary",)),
        name="sqrelu_mlp",
    )(x1, sc2, sh2, gt2, g_pre, g_post, wup, wdn)


RING_SLOTS = 3
RING_AHEAD = RING_SLOTS - 1


def _ring_step(step, total, copies):
    def start_all(st, slot):
        for n, cp in enumerate(copies(st, slot)):
            cp.start(priority=n % 2)

    for k in range(RING_AHEAD):
        @pl.when(jnp.logical_and(step == 0, k < total))
        def _():
            start_all(k, k)

    @pl.when(step + RING_AHEAD < total)
    def _():
        start_all(step + RING_AHEAD, (step + RING_AHEAD) % RING_SLOTS)

    slot = step % RING_SLOTS
    for cp in copies(step, slot):
        cp.wait()
    return slot


def _select_sample_kernel(pc, pt_ref, q_ref, ck_ref, sel_ref, buf, kmt, sem):
    step = pl.program_id(0)
    n_pages = pt_ref.shape[1]
    nch = n_pages // pc
    nb = n_pages // 2

    def copies(st, slot):
        b, c = st // nch, st % nch
        return [pltpu.make_async_copy(ck_ref.at[pt_ref[b, c * pc + p]], buf.at[slot, p], sem.at[slot])
                for p in range(pc)]

    slot = _ring_step(step, pl.num_programs(0), copies)
    c = step % nch

    @pl.when(c == 0)
    def _():
        kmt[...] = jnp.zeros(kmt.shape, F32)

    lane = _lane_iota((DH_A, LANES))
    for hh in range(H_A):
        rows = slice((hh % 2) * DH_A, (hh % 2 + 1) * DH_A)
        acc = kmt[hh // 2, rows, :]
        for j in range(pc // 2):
            pair = buf[slot, 2 * j, hh] + buf[slot, 2 * j + 1, hh]
            col = jnp.sum(pair, axis=1, keepdims=True) * (1.0 / MOBA_BLOCK)
            acc = jnp.where(lane == c * (pc // 2) + j, col, acc)
        kmt[hh // 2, rows, :] = acc

    @pl.when(c == nch - 1)
    def _():
        q = q_ref[0]
        rows = q.shape[0]
        lane_q = _lane_iota((rows, LANES))
        scores = []
        for hh in range(H_A):
            sl = slice((hh // 2) * LANES, (hh // 2 + 1) * LANES)
            half_mask = (lane_q < DH_A) if hh % 2 == 0 else (lane_q >= DH_A)
            scores.append(jnp.dot(jnp.where(half_mask, q[:, sl], 0.0), kmt[hh // 2],
                                  precision=lax.Precision.HIGHEST, preferred_element_type=F32))
        scores = jnp.concatenate(scores, axis=0)
        _, idxs = _top3_lanes(scores, _lane_iota(scores.shape) < nb)
        out = jnp.zeros((rows, LANES), jnp.int32)
        for hh in range(H_A):
            for r, idx in enumerate(idxs):
                out = jnp.where(lane_q == hh * MOBA_TOPK + r, idx[hh * rows:(hh + 1) * rows], out)
        sel_ref[0] = out


def _select_sample(page_table, q_pad, ck_t):
    db, n_pages = page_table.shape
    rows = q_pad.shape[1]
    pc = next(c for c in (16, 8, 4, 2) if n_pages % c == 0)
    nch = n_pages // pc
    return pl.pallas_call(
        functools.partial(_select_sample_kernel, pc),
        grid_spec=pltpu.PrefetchScalarGridSpec(
            num_scalar_prefetch=1, grid=(db * nch,),
            in_specs=[pl.BlockSpec((1, rows, 512), lambda s, pt: (s // nch, 0, 0)),
                      pl.BlockSpec(memory_space=pl.ANY)],
            out_specs=pl.BlockSpec((1, rows, LANES), lambda s, pt: (s // nch, 0, 0)),
            scratch_shapes=[pltpu.VMEM((RING_SLOTS, pc, H_A, DH_A, PAGE_SIZE), F32),
                            pltpu.VMEM((H_A // 2, 2 * DH_A, LANES), F32),
                            pltpu.SemaphoreType.DMA((RING_SLOTS,))]),
        out_shape=jax.ShapeDtypeStruct((db, rows, LANES), jnp.int32),
        compiler_params=_cparams(("arbitrary",)),
        name="moba_select_sample",
    )(page_table, q_pad, ck_t)


def _moba_sample_kernel(n_tok, pt_ref, sel_ref, q_ref, kn_ref, vn_ref, ck_ref, cv_ref, o_ref,
                        kbuf, vbuf, acc, sem):
    step = pl.program_id(0)
    t = step % n_tok

    def copies(st, slot):
        b, tt = st // n_tok, st % n_tok
        out = []
        for hh in range(H_A):
            for n in range(MOBA_TOPK):
                blk = sel_ref[b, tt * (H_A * MOBA_TOPK) + hh * MOBA_TOPK + n]
                for pg in range(2):
                    phys = pt_ref[b, blk * 2 + pg]
                    dst = pl.ds((n * 2 + pg) * PAGE_SIZE, PAGE_SIZE)
                    out.append(pltpu.make_async_copy(ck_ref.at[phys, hh], kbuf.at[slot, hh, :, dst],
                                                     sem.at[0, slot]))
                    out.append(pltpu.make_async_copy(cv_ref.at[phys, hh], vbuf.at[slot, hh, :, dst],
                                                     sem.at[1, slot]))
        return out

    slot = _ring_step(step, pl.num_programs(0), copies)

    @pl.when(t == 0)
    def _():
        acc[...] = jnp.zeros(acc.shape, F32)

    scale = DH_A ** -0.5
    q = q_ref[0]
    kn = kn_ref[0]
    vn = vn_ref[0]
    rows = q.shape[0]
    r_i = lax.broadcasted_iota(jnp.int32, (rows, rows), 0)
    c_i = lax.broadcasted_iota(jnp.int32, (rows, rows), 1)
    own_ok = jnp.logical_and(c_i <= r_i, c_i < n_tok)
    row_is_t = lax.broadcasted_iota(jnp.int32, (rows, DH_A), 0) == t
    for hh in range(H_A):
        sl = slice(hh * DH_A, (hh + 1) * DH_A)
        qh = (q[:, sl] * scale).astype(BF16)
        ks_t = kbuf[slot, hh].astype(BF16)
        vs_t = vbuf[slot, hh].astype(BF16)
        s = _dot(qh, ks_t)
        s_own = jnp.where(own_ok, _dot_nt(qh, kn[:, sl].astype(BF16)), NEG_INF)
        m = jnp.maximum(jnp.max(s, axis=1, keepdims=True), jnp.max(s_own, axis=1, keepdims=True))
        p = jnp.exp(s - m)
        p_own = jnp.exp(s_own - m)
        l = jnp.sum(p, axis=1, keepdims=True) + jnp.sum(p_own, axis=1, keepdims=True)
        o = (_dot_nt(p.astype(BF16), vs_t) + _dot(p_own.astype(BF16), vn[:, sl].astype(BF16))) / l
        acc[:, sl] = jnp.where(row_is_t, o, acc[:, sl])
    o_ref[0] = acc[...]


def _moba_sample(page_table, sel, q_pad, kn_pad, vn_pad, ck_t, cv_t, n_tok):
    db, rows, _ = q_pad.shape
    n_keys = MOBA_TOPK * 2 * PAGE_SIZE
    tok = lambda: pl.BlockSpec((1, rows, 512), lambda s, pt, sl: (s // n_tok, 0, 0))
    return pl.pallas_call(
        functools.partial(_moba_sample_kernel, n_tok),
        grid_spec=pltpu.PrefetchScalarGridSpec(
            num_scalar_prefetch=2, grid=(db * n_tok,),
            in_specs=[tok(), tok(), tok(), pl.BlockSpec(memory_space=pl.ANY),
                      pl.BlockSpec(memory_space=pl.ANY)],
            out_specs=tok(),
            scratch_shapes=[pltpu.VMEM((RING_SLOTS, H_A, DH_A, n_keys), F32),
                            pltpu.VMEM((RING_SLOTS, H_A, DH_A, n_keys), F32),
                            pltpu.VMEM((rows, 512), F32),
                            pltpu.SemaphoreType.DMA((2, RING_SLOTS))]),
        out_shape=jax.ShapeDtypeStruct((db, rows, 512), F32),
        compiler_params=_cparams(("arbitrary",)),
        name="moba_attn_sample",
    )(page_table, sel, q_pad, kn_pad, vn_pad, ck_t, cv_t)


def _mla_sample_kernel(n_tok, pages, pt_ref, q_ref, kn_ref, cl_ref, cr_ref, o_ref,
                       lbuf, rbuf, m_sc, l_sc, acc_sc, sem):
    step = pl.program_id(0)
    nc = pt_ref.shape[1] // pages
    c = step % nc

    def copies(st, slot):
        b, cc = st // nc, st % nc
        out = []
        for p in range(pages):
            phys = pt_ref[b, cc * pages + p]
            dst = pl.ds(p * PAGE_SIZE, PAGE_SIZE)
            out.append(pltpu.make_async_copy(cl_ref.at[phys], lbuf.at[slot, dst, :], sem.at[0, slot]))
            out.append(pltpu.make_async_copy(cr_ref.at[phys], rbuf.at[slot, :, dst], sem.at[1, slot]))
        return out

    slot = _ring_step(step, pl.num_programs(0), copies)

    @pl.when(c == 0)
    def _():
        m_sc[...] = jnp.full(m_sc.shape, -jnp.inf, F32)
        l_sc[...] = jnp.zeros(l_sc.shape, F32)
        acc_sc[...] = jnp.zeros(acc_sc.shape, F32)

    q = q_ref[0]
    q_lat, q_rope = q[:, :KV_RANK], q[:, KV_RANK:KV_RANK + ROPE_B]

    def update(s, v):
        m_prev = m_sc[...]
        m_next = jnp.maximum(m_prev, jnp.max(s, axis=1, keepdims=True))
        alpha = jnp.exp2(m_prev - m_next)
        p = jnp.exp2(s - m_next[:, :1])
        l_sc[...] = l_sc[...] * alpha + jnp.sum(p, axis=1, keepdims=True)
        acc_sc[...] = acc_sc[...] * alpha + _dot(p.astype(BF16), v)
        m_sc[...] = m_next

    lat = lbuf[slot].astype(BF16)
    s = _dot_nt(q_lat, lat) + _dot(q_rope, rbuf[slot].astype(BF16))
    update(s, lat)

    @pl.when(c == nc - 1)
    def _():
        kn = kn_ref[0]
        rows, n_new = q.shape[0], kn.shape[0]
        s_new = _dot_nt(q_lat, kn[:, :KV_RANK]) + _dot_nt(q_rope, kn[:, KV_RANK:KV_RANK + ROPE_B])
        tok = lax.broadcasted_iota(jnp.int32, (rows, n_new), 0) // H_B
        col = lax.broadcasted_iota(jnp.int32, (rows, n_new), 1)
        s_new = jnp.where(jnp.logical_and(col <= tok, col < n_tok), s_new, NEG_INF)
        update(s_new, kn[:, :KV_RANK])
        o_ref[0] = (acc_sc[...] / l_sc[...]).astype(o_ref.dtype)


def _mla_sample(page_table, q_rows, kn_pad, cache_latent, ckr_t, n_tok):
    db, rows, dk = q_rows.shape
    n_pages = page_table.shape[1]
    pages = next(c for c in (64, 32, 16, 8, 4, 2, 1) if n_pages % c == 0)
    nc = n_pages // pages
    return pl.pallas_call(
        functools.partial(_mla_sample_kernel, n_tok, pages),
        grid_spec=pltpu.PrefetchScalarGridSpec(
            num_scalar_prefetch=1, grid=(db * nc,),
            in_specs=[pl.BlockSpec((1, rows, dk), lambda s, pt: (s // nc, 0, 0)),
                      pl.BlockSpec((1,) + kn_pad.shape[1:], lambda s, pt: (s // nc, 0, 0)),
                      pl.BlockSpec(memory_space=pl.ANY), pl.BlockSpec(memory_space=pl.ANY)],
            out_specs=pl.BlockSpec((1, rows, KV_RANK), lambda s, pt: (s // nc, 0, 0)),
            scratch_shapes=[pltpu.VMEM((RING_SLOTS, pages * PAGE_SIZE, KV_RANK), F32),
                            pltpu.VMEM((RING_SLOTS, ROPE_B, pages * PAGE_SIZE), F32),
                            pltpu.VMEM((rows, LANES), F32), pltpu.VMEM((rows, LANES), F32),
                            pltpu.VMEM((rows, KV_RANK), F32),
                            pltpu.SemaphoreType.DMA((2, RING_SLOTS))]),
        out_shape=jax.ShapeDtypeStruct((db, rows, KV_RANK), BF16),
        compiler_params=_cparams(("arbitrary",)),
        name="mla_attn_sample",
    )(page_table, q_rows, kn_pad, cache_latent, ckr_t)


def _prepare_weights(w_in, w_uq, w_uk, w_uv, w_proj_a, w_proj_b, w_out, w_up, w_down):
    o = 3 * H_A * DH_A
    segs = [w_in[:, :o + Q_RANK + KV_RANK], w_in[:, o + Q_RANK + KV_RANK + ROPE_B:],
            w_in[:, o + Q_RANK + KV_RANK:o + Q_RANK + KV_RANK + ROPE_B],
            jnp.zeros((D_MODEL, LANES - ROPE_B), w_in.dtype)]
    w_in_p = jnp.concatenate(segs, axis=1).astype(BF16)
    wq = w_uq.reshape(Q_RANK, H_B, NOPE_B + ROPE_B)
    wuq_n = wq[:, :, :NOPE_B].reshape(Q_RANK, H_B * NOPE_B).astype(BF16)
    wuq_r = jnp.pad(wq[:, :, NOPE_B:], ((0, 0), (0, 0), (0, LANES - ROPE_B)))
    wuq_r = wuq_r.reshape(Q_RANK, H_B * LANES).astype(BF16)
    eye = jnp.eye(H_B, dtype=w_uk.dtype)
    wuk_bd = jnp.einsum('rhn,hg->hngr', w_uk, eye).reshape(H_B * NOPE_B, H_B * KV_RANK).astype(BF16)
    wuv_bd = jnp.einsum('rhv,hg->hrgv', w_uv, eye).reshape(H_B * KV_RANK, H_B * V_B).astype(BF16)
    return dict(w_in_p=w_in_p, wuq_n=wuq_n, wuq_r=wuq_r, wuk_bd=wuk_bd, wuv_bd=wuv_bd,
                wpa=w_proj_a.astype(BF16), wpb=w_proj_b.astype(BF16), wout=w_out.astype(BF16),
                wup=w_up.astype(BF16), wdn=w_down.astype(BF16))


def _row(g):
    return g.reshape(1, -1)


def _trunk(x2, mods, pos, attend, prompt, wts, g_pre_mix, g_post_mix, g_pre_mlp, g_post_mlp,
           g_q_lat, g_kv_lat):
    sh1, sc1, gt1, sh2, sc2, gt2 = mods
    tabs_a = _rope_tables(pos, ROT_A, THETA_A, DH_A)
    tabs_b = _rope_tables(pos, ROPE_B, THETA_B, ROPE_B)
    outs = _proj_in(x2, sc1, sh1, _row(g_pre_mix), wts['w_in_p'], _row(g_q_lat), _row(g_kv_lat),
                    wts['wuq_n'], wts['wuq_r'], wts['wuk_bd'], tabs_a, tabs_b, prompt)
    qa, k, v, latent, krope, ga, gb, qcat, kcat = outs[:9]
    oa, ob = attend(qa, k, v, qcat, kcat, outs[9:])
    x1 = _merge(oa, ob, ga, gb, x2, gt1, _row(g_post_mix), wts['wpa'], wts['wuv_bd'],
                wts['wpb'], wts['wout'])
    y = _mlp(x1, sc2, sh2, gt2, _row(g_pre_mlp), _row(g_post_mlp), wts['wup'], wts['wdn'])
    return y, k, v, latent, krope


def kernel(x_prompt, x_sample, c_prompt, c_sample, cache_k, cache_v, cache_latent, cache_krope, page_table, w_ada, b_ada, g_pre_mix, g_post_mix, g_pre_mlp, g_post_mlp, w_in, g_q_lat, g_kv_lat, w_uq, w_uk, w_uv, w_proj_a, w_proj_b, w_out, w_up, w_down):
    bsz, seq, _ = x_prompt.shape
    db, n_tok, _ = x_sample.shape
    n_pages = page_table.shape[1]
    past = n_pages * PAGE_SIZE
    assert bsz == 1 and seq % ROW_TILE == 0 and seq // MOBA_BLOCK <= N_SEL
    assert (db * n_tok) % ROW_TILE == 0 and n_tok <= 8
    assert past % MOBA_BLOCK == 0 and MOBA_TOPK <= past // MOBA_BLOCK <= LANES

    wts = _prepare_weights(w_in, w_uq, w_uk, w_uv, w_proj_a, w_proj_b, w_out, w_up, w_down)
    gains = (g_pre_mix, g_post_mix, g_pre_mlp, g_post_mlp, g_q_lat, g_kv_lat)

    n_c = bsz + db
    c_all = jnp.pad(jnp.concatenate([c_prompt, c_sample], axis=0), ((0, -n_c % 8), (0, 0)))
    mod_all = _modulation(c_all, w_ada, b_ada)
    mods_p = tuple(jnp.split(mod_all[:bsz], 6, axis=1))
    mods_s = tuple(jnp.split(jnp.repeat(mod_all[bsz:n_c], n_tok, axis=0), 6, axis=1))

    def attend_prompt(qa, k, v, qcat, kcat, extra):
        kaug, vaug, kmean = extra
        nb = kmean.shape[0]
        kmean_ext = jnp.pad(kmean.reshape(nb, 512), ((N_SEL, LANES - N_SEL - nb), (0, 0)))
        qaug = _select_prompt(qa, kmean_ext)
        return _moba_prompt(qaug, kaug, vaug), _mla_prompt(qcat, kcat)

    pos_p = jnp.arange(seq, dtype=jnp.int32)
    y_p, k_p, v_p, lat_p, kr_p = _trunk(x_prompt.reshape(seq, D_MODEL), mods_p, pos_p, attend_prompt,
                                        True, wts, *gains)

    rows_pad = 8
    ck_t = cache_k.transpose(0, 2, 3, 1)
    cv_t = cache_v.transpose(0, 2, 3, 1)
    ckr_t = cache_krope.transpose(0, 2, 1)

    def attend_sample(qa, k, v, qcat, kcat, extra):
        pad_tok = lambda a: jnp.pad(a.reshape(db, n_tok, a.shape[-1]),
                                    ((0, 0), (0, rows_pad - n_tok), (0, 0)))
        q_pad, kn_pad, vn_pad = pad_tok(qa), pad_tok(k), pad_tok(v)
        sel = _select_sample(page_table, q_pad, ck_t)
        sel = sel[:, :n_tok, :H_A * MOBA_TOPK].reshape(db, n_tok * H_A * MOBA_TOPK)
        oa = _moba_sample(page_table, sel, q_pad, kn_pad, vn_pad, ck_t, cv_t, n_tok)
        oa = oa[:, :n_tok].reshape(db * n_tok, H_A // 2, LANES).transpose(1, 0, 2).astype(BF16)
        q_rows = qcat.reshape(H_B, db, n_tok, 2 * LANES).transpose(1, 2, 0, 3)
        q_rows = q_rows.reshape(db, n_tok * H_B, 2 * LANES)
        ob = _mla_sample(page_table, q_rows, pad_tok(kcat), cache_latent, ckr_t, n_tok)
        return oa, ob.reshape(db * n_tok, H_B * KV_RANK)

    pos_s = past + jnp.tile(jnp.arange(n_tok, dtype=jnp.int32), db)
    y_s, k_s, v_s, lat_s, kr_s = _trunk(x_sample.reshape(db * n_tok, D_MODEL), mods_s, pos_s,
                                        attend_sample, False, wts, *gains)

    return (y_p.reshape(bsz, seq, D_MODEL), y_s.reshape(db, n_tok, D_MODEL),
            k_p.reshape(bsz, seq, H_A, DH_A), v_p.reshape(bsz, seq, H_A, DH_A),
            lat_p.reshape(bsz, seq, KV_RANK), kr_p.reshape(bsz, seq, ROPE_B),
            k_s.reshape(db, n_tok, H_A, DH_A), v_s.reshape(db, n_tok, H_A, DH_A),
            lat_s.reshape(db, n_tok, KV_RANK), kr_s.reshape(db, n_tok, ROPE_B))
```

```python
import functools

import jax
import jax.numpy as jnp
from jax import lax
from jax.experimental import pallas as pl
from jax.experimental.pallas import tpu as pltpu

F32 = jnp.float32
BF16 = jnp.bfloat16

D_MODEL = 1024
H_A = 8
DH_A = 64
ROT_A = DH_A // 4
THETA_A = 500000.0
MOBA_BLOCK = 256
MOBA_TOPK = 3
H_B = 8
Q_RANK = 256
KV_RANK = 128
NOPE_B = 64
ROPE_B = 32
V_B = 64
THETA_B = 10000.0
D_FF = 4 * D_MODEL
PAGE_SIZE = 128
EPS = 1e-6
NEG_INF = -1e30
LOG2E = 1.4426950408889634

LANES = 128
ROW_TILE = 256
N_SEL = 64
VMEM_LIMIT = 56 * 1024 * 1024

C_QA, C_KA, C_VA = 0, 512, 1024
C_QLAT, C_KVLAT = 1536, 1792
C_GA, C_GB, C_KR = 1920, 2944, 3968
D_IN_PAD = 4096
ONES_LANE_A = DH_A
ONES_LANE_B = ROPE_B


def _cparams(sem, vmem=VMEM_LIMIT):
    return pltpu.CompilerParams(dimension_semantics=sem, vmem_limit_bytes=vmem)


def _rms(x, g):
    return x * lax.rsqrt(jnp.mean(x * x, axis=-1, keepdims=True) + EPS) * g


def _dot(a, b):
    return jnp.dot(a, b, preferred_element_type=F32)


def _dot_nt(a, b, precision=None):
    return lax.dot_general(a, b, (((1,), (1,)), ((), ())), precision=precision,
                           preferred_element_type=F32)


def _lane_iota(shape):
    return lax.broadcasted_iota(jnp.int32, shape, len(shape) - 1)


def _rope_slab(x, c, s_up, s_dn, half):
    return x * c + pltpu.roll(x, half, 1) * s_up + pltpu.roll(x, LANES - half, 1) * s_dn


def _rope_tables(pos, rot, theta, group):
    half = rot // 2
    inv = 1.0 / (theta ** (jnp.arange(half, dtype=F32) / half))
    ang = pos.astype(F32)[:, None] * inv[None, :]
    cos, sin = jnp.cos(ang), jnp.sin(ang)
    n = pos.shape[0]
    z_half = jnp.zeros((n, half), F32)
    rest1 = jnp.ones((n, group - rot), F32)
    rest0 = jnp.zeros((n, group - rot), F32)
    c = jnp.concatenate([cos, cos, rest1], axis=1)
    s_up = jnp.concatenate([z_half, sin, rest0], axis=1)
    s_dn = jnp.concatenate([-sin, z_half, rest0], axis=1)
    rep = LANES // group
    return tuple(jnp.tile(t, (1, rep)) for t in (c, s_up, s_dn))


def _mod_kernel(c_ref, w_ref, b_ref, o_ref):
    c = c_ref[...]
    a = (c * jax.nn.sigmoid(c)).astype(BF16)
    o_ref[...] = _dot(a, w_ref[...].astype(BF16)) + b_ref[...]


def _modulation(c_all, w_ada, b_ada):
    rows = c_all.shape[0]
    n = w_ada.shape[1]
    tn = 768
    return pl.pallas_call(
        _mod_kernel,
        grid=(n // tn,),
        in_specs=[pl.BlockSpec((rows, D_MODEL), lambda j: (0, 0)),
                  pl.BlockSpec((D_MODEL, tn), lambda j: (0, j)),
                  pl.BlockSpec((1, tn), lambda j: (0, j))],
        out_specs=pl.BlockSpec((rows, tn), lambda j: (0, j)),
        out_shape=jax.ShapeDtypeStruct((rows, n), F32),
        compiler_params=_cparams(("arbitrary",)),
        name="adaln_modulation",
    )(c_all, w_ada, b_ada.reshape(1, n))


def _proj_in_kernel(prompt, x_ref, sc_ref, sh_ref, g_ref, w_ref, gq_ref, gkv_ref, wuqn_ref, wuqr_ref,
                    wuk_ref, ca_ref, sau_ref, sad_ref, cb_ref, sbu_ref, sbd_ref,
                    qa_ref, k_ref, v_ref, lat_ref, kr_ref, ga_ref, gb_ref, qcat_ref, kcat_ref,
                    *moba_refs):
    tm = x_ref.shape[0]
    x = x_ref[...]
    h = _rms(x, g_ref[...]) * (1.0 + sc_ref[...]) + sh_ref[...]
    p = _dot(h.astype(BF16), w_ref[...])

    ca, sau, sad = ca_ref[...], sau_ref[...], sad_ref[...]
    cb, sbu, sbd = cb_ref[...], sbu_ref[...], sbd_ref[...]

    def rope_a(y):
        return jnp.concatenate(
            [_rope_slab(y[:, s * LANES:(s + 1) * LANES], ca, sau, sad, ROT_A // 2)
             for s in range(H_A * DH_A // LANES)], axis=1)

    qa = rope_a(p[:, C_QA:C_QA + 512])
    ka = rope_a(p[:, C_KA:C_KA + 512])
    va = p[:, C_VA:C_VA + 512]
    qa_ref[...] = qa
    k_ref[...] = ka
    v_ref[...] = va
    ga_ref[...] = p[:, C_GA:C_GA + D_MODEL]
    gb_ref[...] = p[:, C_GB:C_GB + D_MODEL]

    latent = _rms(p[:, C_KVLAT:C_KVLAT + KV_RANK], gkv_ref[...])
    lat_ref[...] = latent
    kr = _rope_slab(p[:, C_KR:C_KR + LANES], cb, sbu, sbd, ROPE_B // 2)
    kr_ref[...] = kr[:, :ROPE_B]
    lane = _lane_iota((tm, LANES))
    kr_slab = jnp.where(lane == ONES_LANE_B, 1.0, kr)
    kcat_ref[...] = jnp.concatenate([latent, kr_slab], axis=1).astype(BF16)

    scale_b = (NOPE_B + ROPE_B) ** -0.5 * LOG2E
    qn = _rms(p[:, C_QLAT:C_QLAT + Q_RANK], gq_ref[...]).astype(BF16)
    q_nope = _dot(qn, wuqn_ref[...])
    q_rope = _dot(qn, wuqr_ref[...])
    q_abs = _dot(q_nope.astype(BF16), wuk_ref[...])
    for hh in range(H_B):
        sl = slice(hh * LANES, (hh + 1) * LANES)
        qr = _rope_slab(q_rope[:, sl], cb, sbu, sbd, ROPE_B // 2)
        qcat_ref[hh] = (jnp.concatenate([q_abs[:, sl], qr], axis=1) * scale_b).astype(BF16)

    if prompt:
        kaug_ref, vaug_ref, kmean_ref = moba_refs
        blk = pl.program_id(0)
        onehot = jnp.where(lane == N_SEL + blk, 1.0, 0.0)
        ones_col = jnp.where(lane == ONES_LANE_A, 1.0, 0.0)
        for hh in range(H_A):
            sl = slice((hh // 2) * LANES, (hh // 2 + 1) * LANES)
            ks, vs = ka[:, sl], va[:, sl]
            if hh % 2 == 1:
                ks = pltpu.roll(ks, DH_A, 1)
                vs = pltpu.roll(vs, DH_A, 1)
            kaug_ref[hh] = jnp.where(lane < DH_A, ks, onehot).astype(BF16)
            vaug_ref[hh] = jnp.where(lane < DH_A, vs, ones_col).astype(BF16)
        kmean_ref[0] = jnp.sum(ka, axis=0, keepdims=True) * (1.0 / MOBA_BLOCK)


def _proj_in(x2, sc, sh, g_pre, w_in_p, g_q, g_kv, wuq_n, wuq_r, wuk_bd, tabs_a, tabs_b, prompt):
    rows = x2.shape[0]
    tm = ROW_TILE
    nt = rows // tm
    row = lambda w: pl.BlockSpec((tm, w), lambda i: (i, 0))
    const = lambda a: pl.BlockSpec(a.shape, lambda i: (0,) * a.ndim)
    mod = (lambda a: const(a)) if sc.shape[0] == 1 else (lambda a: row(a.shape[1]))
    heads = lambda w: pl.BlockSpec((H_A, tm, w), lambda i: (0, i, 0))
    in_specs = [row(D_MODEL), mod(sc), mod(sh), const(g_pre), const(w_in_p), const(g_q), const(g_kv),
                const(wuq_n), const(wuq_r), const(wuk_bd)] + [row(LANES)] * 6
    out_shape = [jax.ShapeDtypeStruct((rows, 512), F32)] * 3 + [
        jax.ShapeDtypeStruct((rows, KV_RANK), F32), jax.ShapeDtypeStruct((rows, ROPE_B), F32),
        jax.ShapeDtypeStruct((rows, D_MODEL), F32), jax.ShapeDtypeStruct((rows, D_MODEL), F32),
        jax.ShapeDtypeStruct((H_B, rows, 2 * LANES), BF16), jax.ShapeDtypeStruct((rows, 2 * LANES), BF16)]
    out_specs = [row(512)] * 3 + [row(KV_RANK), row(ROPE_B), row(D_MODEL), row(D_MODEL),
                                  heads(2 * LANES), row(2 * LANES)]
    if prompt:
        out_shape += [jax.ShapeDtypeStruct((H_A, rows, LANES), BF16)] * 2 + [
            jax.ShapeDtypeStruct((nt, 1, 512), F32)]
        out_specs += [heads(LANES), heads(LANES), pl.BlockSpec((1, 1, 512), lambda i: (i, 0, 0))]
    return pl.pallas_call(
        functools.partial(_proj_in_kernel, prompt),
        grid=(nt,), in_specs=in_specs, out_specs=out_specs, out_shape=out_shape,
        compiler_params=_cparams(("arbitrary",)),
        name="proj_in_prompt" if prompt else "proj_in_sample",
    )(x2, sc, sh, g_pre, w_in_p, g_q, g_kv, wuq_n, wuq_r, wuk_bd, *tabs_a, *tabs_b)


def _top3_lanes(scores, valid):
    lane = _lane_iota(scores.shape)
    s = jnp.where(valid, scores, NEG_INF)
    picked = jnp.zeros(scores.shape, jnp.bool_)
    idxs = []
    for _ in range(MOBA_TOPK):
        mx = jnp.max(s, axis=1, keepdims=True)
        idx = jnp.min(jnp.where(s == mx, lane, 2 * LANES), axis=1, keepdims=True)
        hit = lane == idx
        picked = jnp.logical_or(picked, hit)
        s = jnp.where(hit, -jnp.inf, s)
        idxs.append(idx)
    return picked, idxs


def _select_prompt_kernel(qa_ref, kmean_ref, qaug_ref):
    tm = qa_ref.shape[0]
    own = pl.program_id(0)
    lane = _lane_iota((tm, LANES))
    scale = DH_A ** -0.5 * LOG2E
    scores = []
    for hh in range(H_A):
        sl = slice((hh // 2) * LANES, (hh // 2 + 1) * LANES)
        half_mask = (lane < DH_A) if hh % 2 == 0 else (lane >= DH_A)
        scores.append(_dot_nt(jnp.where(half_mask, qa_ref[:, sl], 0.0), kmean_ref[:, sl],
                              precision=lax.Precision.HIGHEST))
    scores = jnp.concatenate(scores, axis=0)
    blk = _lane_iota(scores.shape) - N_SEL
    valid = jnp.logical_and(blk >= 0, blk < own)
    picked, _ = _top3_lanes(scores, valid)
    keep = jnp.logical_or(jnp.logical_and(picked, valid), blk == own)
    bias = jnp.where(keep, 0.0, NEG_INF)
    for hh in range(H_A):
        qs = qa_ref[:, (hh // 2) * LANES:(hh // 2 + 1) * LANES]
        qh = qs if hh % 2 == 0 else pltpu.roll(qs, DH_A, 1)
        qaug_ref[hh] = jnp.where(lane < DH_A, qh * scale, bias[hh * tm:(hh + 1) * tm]).astype(BF16)


def _select_prompt(qa, kmean_ext):
    rows = qa.shape[0]
    tm = ROW_TILE
    return pl.pallas_call(
        _select_prompt_kernel,
        grid=(rows // tm,),
        in_specs=[pl.BlockSpec((tm, 512), lambda i: (i, 0)),
                  pl.BlockSpec(kmean_ext.shape, lambda i: (0, 0))],
        out_specs=pl.BlockSpec((H_A, tm, LANES), lambda i: (0, i, 0)),
        out_shape=jax.ShapeDtypeStruct((H_A, rows, LANES), BF16),
        compiler_params=_cparams(("arbitrary",)),
        name="moba_select_prompt",
    )(qa, kmean_ext)


def _softmax_step(s, v, m_sc, acc_sc):
    reps = s.shape[1] // LANES
    m_prev = m_sc[...]
    m_next = jnp.maximum(m_prev, jnp.max(s, axis=1, keepdims=True))
    alpha = jnp.exp2(m_prev - m_next)
    p = jnp.exp2(s - jnp.tile(m_next, (1, reps)))
    acc = acc_sc[...]
    acc_sc[...] = acc * jnp.tile(alpha, (1, acc.shape[1] // LANES)) + _dot(p.astype(BF16), v)
    m_sc[...] = m_next


def _moba_prompt_kernel(q_ref, k_ref, v_ref, o_ref, sa_sc, sb_sc, m_sc, acc_sc):
    hb = q_ref.shape[0]
    own = pl.program_id(1)
    m_sc[...] = jnp.full(m_sc.shape, -jnp.inf, F32)
    acc_sc[...] = jnp.zeros(acc_sc.shape, F32)

    def rows_of(ref, hh, blk):
        return ref[hh, pl.ds(pl.multiple_of(blk * MOBA_BLOCK, MOBA_BLOCK), MOBA_BLOCK), :]

    def scores(hh, blk):
        return _dot_nt(q_ref[hh], rows_of(k_ref, hh, blk))

    def consume(cur, blk):
        for hh in range(hb):
            _softmax_step(cur[hh], rows_of(v_ref, hh, blk), m_sc.at[hh], acc_sc.at[hh])

    def stage(cur, nxt, blk_next, blk_cur):
        for hh in range(hb):
            nxt[hh] = scores(hh, blk_next)
        consume(cur, blk_cur)

    for hh in range(hb):
        s = scores(hh, own)
        r = lax.broadcasted_iota(jnp.int32, s.shape, 0)
        c = lax.broadcasted_iota(jnp.int32, s.shape, 1)
        sa_sc[hh] = jnp.where(c <= r, s, NEG_INF)

    def body(jj, carry):
        j0 = 2 * jj
        stage(sa_sc, sb_sc, j0, jnp.where(jj == 0, own, j0 - 1))
        stage(sb_sc, sa_sc, j0 + 1, j0)
        return carry

    lax.fori_loop(0, own // 2, body, 0)

    @pl.when(own % 2 == 1)
    def _():
        stage(sa_sc, sb_sc, own - 1, jnp.where(own == 1, own, own - 2))
        consume(sb_sc, own - 1)

    @pl.when(own % 2 == 0)
    def _():
        consume(sa_sc, jnp.maximum(own - 1, 0))

    lane = _lane_iota(acc_sc.shape[1:])
    for pr in range(hb // 2):
        a0, a1 = acc_sc[2 * pr], acc_sc[2 * pr + 1]
        o0 = a0 / a0[:, ONES_LANE_A:ONES_LANE_A + 1]
        o1 = a1 / a1[:, ONES_LANE_A:ONES_LANE_A + 1]
        o_ref[pr] = jnp.where(lane < DH_A, o0, pltpu.roll(o1, DH_A, 1)).astype(o_ref.dtype)


MOBA_HEADS_PER_STEP = 4


def _moba_prompt(qaug, kaug, vaug):
    _, rows, _ = qaug.shape
    tq = MOBA_BLOCK
    hb = MOBA_HEADS_PER_STEP
    resident = lambda: pl.BlockSpec((hb, rows, LANES), lambda g, i: (g, 0, 0),
                                    pipeline_mode=pl.Buffered(1))
    return pl.pallas_call(
        _moba_prompt_kernel,
        grid=(H_A // hb, rows // tq),
        in_specs=[pl.BlockSpec((hb, tq, LANES), lambda g, i: (g, i, 0)), resident(), resident()],
        out_specs=pl.BlockSpec((hb // 2, tq, LANES), lambda g, i: (g, i, 0)),
        out_shape=jax.ShapeDtypeStruct((H_A // 2, rows, LANES), BF16),
        scratch_shapes=[pltpu.VMEM((hb, tq, MOBA_BLOCK), F32), pltpu.VMEM((hb, tq, MOBA_BLOCK), F32),
                        pltpu.VMEM((hb, tq, LANES), F32), pltpu.VMEM((hb, tq, LANES), F32)],
        compiler_params=_cparams(("arbitrary", "arbitrary")),
        name="moba_attn_prompt",
    )(qaug, kaug, vaug)


def _mla_prompt_kernel(q_ref, k_ref, o_ref, m_sc, acc_sc):
    nh, tq, dk = q_ref.shape
    i = pl.program_id(0)
    q = q_ref[...].reshape(nh * tq, dk)
    m_sc[...] = jnp.full(m_sc.shape, -jnp.inf, F32)
    acc_sc[...] = jnp.zeros(acc_sc.shape, F32)

    def tile(j, causal):
        start = pl.multiple_of(j * tq, tq)
        k = k_ref[pl.ds(start, tq), :]
        s = _dot_nt(q, k)
        if causal:
            s3 = s.reshape(nh, tq, tq)
            r = lax.broadcasted_iota(jnp.int32, s3.shape, 1)
            c = lax.broadcasted_iota(jnp.int32, s3.shape, 2)
            s = jnp.where(c <= r, s3, NEG_INF).reshape(nh * tq, tq)
        _softmax_step(s, k, m_sc, acc_sc)

    tile(i, True)

    def body(j, carry):
        tile(j, False)
        return carry

    lax.fori_loop(0, i, body, 0)
    acc = acc_sc[...]
    lane_l = LANES + ONES_LANE_B
    o = acc[:, :KV_RANK] / acc[:, lane_l:lane_l + 1]
    for hh in range(nh):
        o_ref[:, hh * KV_RANK:(hh + 1) * KV_RANK] = o[hh * tq:(hh + 1) * tq, :].astype(o_ref.dtype)


def _mla_prompt(qcat, kcat):
    nh, rows, dk = qcat.shape
    tq = ROW_TILE
    return pl.pallas_call(
        _mla_prompt_kernel,
        grid=(rows // tq,),
        in_specs=[pl.BlockSpec((nh, tq, dk), lambda i: (0, i, 0)),
                  pl.BlockSpec((rows, dk), lambda i: (0, 0), pipeline_mode=pl.Buffered(1))],
        out_specs=pl.BlockSpec((tq, nh * KV_RANK), lambda i: (i, 0)),
        out_shape=jax.ShapeDtypeStruct((rows, nh * KV_RANK), BF16),
        scratch_shapes=[pltpu.VMEM((nh * tq, LANES), F32), pltpu.VMEM((nh * tq, dk), F32)],
        compiler_params=_cparams(("arbitrary",)),
        name="mla_attn_prompt",
    )(qcat, kcat)


def _merge_kernel(oa_ref, ob_ref, ga_ref, gb_ref, x_ref, gt_ref, g_ref, wpa_ref, wuv_ref, wpb_ref,
                  wout_ref, x1_ref):
    o_a = jnp.concatenate([oa_ref[pr] for pr in range(H_A // 2)], axis=1)
    y_a = _dot(o_a, wpa_ref[...])
    o_b = _dot(ob_ref[...], wuv_ref[...])
    y_b = _dot(o_b.astype(BF16), wpb_ref[...])
    merged = jax.nn.sigmoid(ga_ref[...]) * y_a + jax.nn.sigmoid(gb_ref[...]) * y_b
    m = _dot(merged.astype(BF16), wout_ref[...])
    x1_ref[...] = x_ref[...] + gt_ref[...] * _rms(m, g_ref[...])


def _merge(oa, ob, ga, gb, x2, gt1, g_post, wpa, wuv_bd, wpb, wout):
    rows = x2.shape[0]
    tm = ROW_TILE
    row = lambda w: pl.BlockSpec((tm, w), lambda i: (i, 0))
    const = lambda a: pl.BlockSpec(a.shape, lambda i: (0,) * a.ndim)
    mod = const(gt1) if gt1.shape[0] == 1 else row(D_MODEL)
    return pl.pallas_call(
        _merge_kernel,
        grid=(rows // tm,),
        in_specs=[pl.BlockSpec((H_A // 2, tm, LANES), lambda i: (0, i, 0)), row(H_B * KV_RANK),
                  row(D_MODEL), row(D_MODEL), row(D_MODEL), mod, const(g_post), const(wpa),
                  const(wuv_bd), const(wpb), const(wout)],
        out_specs=row(D_MODEL),
        out_shape=jax.ShapeDtypeStruct((rows, D_MODEL), F32),
        compiler_params=_cparams(("arbitrary",)),
        name="mixer_merge",
    )(oa, ob, ga, gb, x2, gt1, g_post, wpa, wuv_bd, wpb, wout)


def _mlp_kernel(x1_ref, sc_ref, sh_ref, gt_ref, gpre_ref, gpost_ref, wup_ref, wdn_ref, y_ref):
    x1 = x1_ref[...]
    h2 = _rms(x1, gpre_ref[...]) * (1.0 + sc_ref[...]) + sh_ref[...]
    u = jnp.maximum(_dot(h2.astype(BF16), wup_ref[...]), 0.0)
    d = _dot((u * u).astype(BF16), wdn_ref[...])
    y_ref[...] = x1 + gt_ref[...] * _rms(d, gpost_ref[...])


def _mlp(x1, sc2, sh2, gt2, g_pre, g_post, wup, wdn):
    rows = x1.shape[0]
    tm = ROW_TILE
    row = lambda w: pl.BlockSpec((tm, w), lambda i: (i, 0))
    const = lambda a: pl.BlockSpec(a.shape, lambda i: (0,) * a.ndim)
    mod = (lambda a: const(a)) if sc2.shape[0] == 1 else (lambda a: row(D_MODEL))
    return pl.pallas_call(
        _mlp_kernel,
        grid=(rows // tm,),
        in_specs=[row(D_MODEL), mod(sc2), mod(sh2), mod(gt2), const(g_pre), const(g_post),
                  const(wup), const(wdn)],
        out_specs=row(D_MODEL),
        out_shape=jax.ShapeDtypeStruct((rows, D_MODEL), F32),
        compiler_params=_cparams(("arbitrary",)),
        name="sqrelu_mlp",
    )(x1, sc2, sh2, gt2, g_pre, g_post, wup, wdn)


RING_SLOTS = 3
RING_AHEAD = RING_SLOTS - 1


def _ring_step(step, total, copies, n_queues=1):
    def start_all(st, slot):
        for n, cp in enumerate(copies(st, slot)):
            cp.start(priority=n % n_queues)

    for k in range(RING_AHEAD):
        @pl.when(jnp.logical_and(step == 0, k < total))
        def _():
            start_all(k, k)

    @pl.when(step + RING_AHEAD < total)
    def _():
        start_all(step + RING_AHEAD, (step + RING_AHEAD) % RING_SLOTS)

    slot = step % RING_SLOTS
    for cp in copies(step, slot):
        cp.wait()
    return slot


def _select_sample_kernel(pc, pt_ref, q_ref, ck_ref, sel_ref, buf, kmt, sem):
    step = pl.program_id(0)
    n_pages = pt_ref.shape[1]
    nch = n_pages // pc
    nb = n_pages // 2

    def copies(st, slot):
        b, c = st // nch, st % nch
        return [pltpu.make_async_copy(ck_ref.at[pt_ref[b, c * pc + p]], buf.at[slot, p], sem.at[slot])
                for p in range(pc)]

    slot = _ring_step(step, pl.num_programs(0), copies)
    c = step % nch

    @pl.when(c == 0)
    def _():
        kmt[...] = jnp.zeros(kmt.shape, F32)

    lane = _lane_iota((DH_A, LANES))
    for hh in range(H_A):
        rows = slice((hh % 2) * DH_A, (hh % 2 + 1) * DH_A)
        acc = kmt[hh // 2, rows, :]
        for j in range(pc // 2):
            pair = buf[slot, 2 * j, hh] + buf[slot, 2 * j + 1, hh]
            col = jnp.sum(pair, axis=1, keepdims=True) * (1.0 / MOBA_BLOCK)
            acc = jnp.where(lane == c * (pc // 2) + j, col, acc)
        kmt[hh // 2, rows, :] = acc

    @pl.when(c == nch - 1)
    def _():
        q = q_ref[0]
        rows = q.shape[0]
        lane_q = _lane_iota((rows, LANES))
        scores = []
        for hh in range(H_A):
            sl = slice((hh // 2) * LANES, (hh // 2 + 1) * LANES)
            half_mask = (lane_q < DH_A) if hh % 2 == 0 else (lane_q >= DH_A)
            scores.append(jnp.dot(jnp.where(half_mask, q[:, sl], 0.0), kmt[hh // 2],
                                  precision=lax.Precision.HIGHEST, preferred_element_type=F32))
        scores = jnp.concatenate(scores, axis=0)
        _, idxs = _top3_lanes(scores, _lane_iota(scores.shape) < nb)
        out = jnp.zeros((rows, LANES), jnp.int32)
        for hh in range(H_A):
            for r, idx in enumerate(idxs):
                out = jnp.where(lane_q == hh * MOBA_TOPK + r, idx[hh * rows:(hh + 1) * rows], out)
        sel_ref[0] = out


def _select_sample(page_table, q_pad, ck_t):
    db, n_pages = page_table.shape
    rows = q_pad.shape[1]
    pc = next(c for c in (16, 8, 4, 2) if n_pages % c == 0)
    nch = n_pages // pc
    return pl.pallas_call(
        functools.partial(_select_sample_kernel, pc),
        grid_spec=pltpu.PrefetchScalarGridSpec(
            num_scalar_prefetch=1, grid=(db * nch,),
            in_specs=[pl.BlockSpec((1, rows, 512), lambda s, pt: (s // nch, 0, 0)),
                      pl.BlockSpec(memory_space=pl.ANY)],
            out_specs=pl.BlockSpec((1, rows, LANES), lambda s, pt: (s // nch, 0, 0)),
            scratch_shapes=[pltpu.VMEM((RING_SLOTS, pc, H_A, DH_A, PAGE_SIZE), F32),
                            pltpu.VMEM((H_A // 2, 2 * DH_A, LANES), F32),
                            pltpu.SemaphoreType.DMA((RING_SLOTS,))]),
        out_shape=jax.ShapeDtypeStruct((db, rows, LANES), jnp.int32),
        compiler_params=_cparams(("arbitrary",)),
        name="moba_select_sample",
    )(page_table, q_pad, ck_t)


def _moba_sample_kernel(n_tok, pt_ref, sel_ref, q_ref, kn_ref, vn_ref, ck_ref, cv_ref, o_ref,
                        kbuf, vbuf, acc, sem):
    step = pl.program_id(0)
    t = step % n_tok

    def copies(st, slot):
        b, tt = st // n_tok, st % n_tok
        out = []
        for hh in range(H_A):
            for n in range(MOBA_TOPK):
                blk = sel_ref[b, tt * (H_A * MOBA_TOPK) + hh * MOBA_TOPK + n]
                for pg in range(2):
                    phys = pt_ref[b, blk * 2 + pg]
                    dst = pl.ds((n * 2 + pg) * PAGE_SIZE, PAGE_SIZE)
                    out.append(pltpu.make_async_copy(ck_ref.at[phys, hh], kbuf.at[slot, hh, :, dst],
                                                     sem.at[0, slot]))
                    out.append(pltpu.make_async_copy(cv_ref.at[phys, hh], vbuf.at[slot, hh, :, dst],
                                                     sem.at[1, slot]))
        return out

    slot = _ring_step(step, pl.num_programs(0), copies, n_queues=2)

    @pl.when(t == 0)
    def _():
        acc[...] = jnp.zeros(acc.shape, F32)

    scale = DH_A ** -0.5
    q = q_ref[0]
    kn = kn_ref[0]
    vn = vn_ref[0]
    rows = q.shape[0]
    r_i = lax.broadcasted_iota(jnp.int32, (rows, rows), 0)
    c_i = lax.broadcasted_iota(jnp.int32, (rows, rows), 1)
    own_ok = jnp.logical_and(c_i <= r_i, c_i < n_tok)
    row_is_t = lax.broadcasted_iota(jnp.int32, (rows, DH_A), 0) == t
    heads = [slice(hh * DH_A, (hh + 1) * DH_A) for hh in range(H_A)]
    s_sel, s_own = [], []
    for hh, sl in enumerate(heads):
        qh = (q[:, sl] * scale).astype(BF16)
        s_sel.append(_dot(qh, kbuf[slot, hh].astype(BF16)))
        s_own.append(jnp.where(own_ok, _dot_nt(qh, kn[:, sl].astype(BF16)), NEG_INF))
    s_sel = jnp.concatenate(s_sel, axis=0)
    s_own = jnp.concatenate(s_own, axis=0)
    m = jnp.maximum(jnp.max(s_sel, axis=1, keepdims=True), jnp.max(s_own, axis=1, keepdims=True))
    p_sel = jnp.exp(s_sel - m)
    p_own = jnp.exp(s_own - m)
    l = jnp.sum(p_sel, axis=1, keepdims=True) + jnp.sum(p_own, axis=1, keepdims=True)
    for hh, sl in enumerate(heads):
        rs = slice(hh * rows, (hh + 1) * rows)
        o = (_dot_nt(p_sel[rs].astype(BF16), vbuf[slot, hh].astype(BF16))
             + _dot(p_own[rs].astype(BF16), vn[:, sl].astype(BF16))) / l[rs]
        acc[:, sl] = jnp.where(row_is_t, o, acc[:, sl])
    o_ref[0] = acc[...]


def _moba_sample(page_table, sel, q_pad, kn_pad, vn_pad, ck_t, cv_t, n_tok):
    db, rows, _ = q_pad.shape
    n_keys = MOBA_TOPK * 2 * PAGE_SIZE
    tok = lambda: pl.BlockSpec((1, rows, 512), lambda s, pt, sl: (s // n_tok, 0, 0))
    return pl.pallas_call(
        functools.partial(_moba_sample_kernel, n_tok),
        grid_spec=pltpu.PrefetchScalarGridSpec(
            num_scalar_prefetch=2, grid=(db * n_tok,),
            in_specs=[tok(), tok(), tok(), pl.BlockSpec(memory_space=pl.ANY),
                      pl.BlockSpec(memory_space=pl.ANY)],
            out_specs=tok(),
            scratch_shapes=[pltpu.VMEM((RING_SLOTS, H_A, DH_A, n_keys), F32),
                            pltpu.VMEM((RING_SLOTS, H_A, DH_A, n_keys), F32),
                            pltpu.VMEM((rows, 512), F32),
                            pltpu.SemaphoreType.DMA((2, RING_SLOTS))]),
        out_shape=jax.ShapeDtypeStruct((db, rows, 512), F32),
        compiler_params=_cparams(("arbitrary",)),
        name="moba_attn_sample",
    )(page_table, sel, q_pad, kn_pad, vn_pad, ck_t, cv_t)


def _mla_sample_kernel(n_tok, pages, pt_ref, q_ref, kn_ref, cl_ref, cr_ref, o_ref,
                       lbuf, rbuf, m_sc, l_sc, acc_sc, sem):
    step = pl.program_id(0)
    nc = pt_ref.shape[1] // pages
    c = step % nc

    def copies(st, slot):
        b, cc = st // nc, st % nc
        out = []
        for p in range(pages):
            phys = pt_ref[b, cc * pages + p]
            dst = pl.ds(p * PAGE_SIZE, PAGE_SIZE)
            out.append(pltpu.make_async_copy(cl_ref.at[phys], lbuf.at[slot, dst, :], sem.at[0, slot]))
            out.append(pltpu.make_async_copy(cr_ref.at[phys], rbuf.at[slot, :, dst], sem.at[1, slot]))
        return out

    slot = _ring_step(step, pl.num_programs(0), copies, n_queues=2)

    @pl.when(c == 0)
    def _():
        m_sc[...] = jnp.full(m_sc.shape, -jnp.inf, F32)
        l_sc[...] = jnp.zeros(l_sc.shape, F32)
        acc_sc[...] = jnp.zeros(acc_sc.shape, F32)

    q = q_ref[0]
    q_lat, q_rope = q[:, :KV_RANK], q[:, KV_RANK:KV_RANK + ROPE_B]

    def update(s, v):
        m_prev = m_sc[...]
        m_next = jnp.maximum(m_prev, jnp.max(s, axis=1, keepdims=True))
        alpha = jnp.exp2(m_prev - m_next)
        p = jnp.exp2(s - m_next[:, :1])
        l_sc[...] = l_sc[...] * alpha + jnp.sum(p, axis=1, keepdims=True)
        acc_sc[...] = acc_sc[...] * alpha + _dot(p.astype(BF16), v)
        m_sc[...] = m_next

    lat = lbuf[slot].astype(BF16)
    s = _dot_nt(q_lat, lat) + _dot(q_rope, rbuf[slot].astype(BF16))
    update(s, lat)

    @pl.when(c == nc - 1)
    def _():
        kn = kn_ref[0]
        rows, n_new = q.shape[0], kn.shape[0]
        s_new = _dot_nt(q_lat, kn[:, :KV_RANK]) + _dot_nt(q_rope, kn[:, KV_RANK:KV_RANK + ROPE_B])
        tok = lax.broadcasted_iota(jnp.int32, (rows, n_new), 0) // H_B
        col = lax.broadcasted_iota(jnp.int32, (rows, n_new), 1)
        s_new = jnp.where(jnp.logical_and(col <= tok, col < n_tok), s_new, NEG_INF)
        update(s_new, kn[:, :KV_RANK])
        o_ref[0] = (acc_sc[...] / l_sc[...]).astype(o_ref.dtype)


def _mla_sample(page_table, q_rows, kn_pad, cache_latent, ckr_t, n_tok):
    db, rows, dk = q_rows.shape
    n_pages = page_table.shape[1]
    pages = next(c for c in (64, 32, 16, 8, 4, 2, 1) if n_pages % c == 0)
    nc = n_pages // pages
    return pl.pallas_call(
        functools.partial(_mla_sample_kernel, n_tok, pages),
        grid_spec=pltpu.PrefetchScalarGridSpec(
            num_scalar_prefetch=1, grid=(db * nc,),
            in_specs=[pl.BlockSpec((1, rows, dk), lambda s, pt: (s // nc, 0, 0)),
                      pl.BlockSpec((1,) + kn_pad.shape[1:], lambda s, pt: (s // nc, 0, 0)),
                      pl.BlockSpec(memory_space=pl.ANY), pl.BlockSpec(memory_space=pl.ANY)],
            out_specs=pl.BlockSpec((1, rows, KV_RANK), lambda s, pt: (s // nc, 0, 0)),
            scratch_shapes=[pltpu.VMEM((RING_SLOTS, pages * PAGE_SIZE, KV_RANK), F32),
                            pltpu.VMEM((RING_SLOTS, ROPE_B, pages * PAGE_SIZE), F32),
                            pltpu.VMEM((rows, LANES), F32), pltpu.VMEM((rows, LANES), F32),
                            pltpu.VMEM((rows, KV_RANK), F32),
                            pltpu.SemaphoreType.DMA((2, RING_SLOTS))]),
        out_shape=jax.ShapeDtypeStruct((db, rows, KV_RANK), BF16),
        compiler_params=_cparams(("arbitrary",)),
        name="mla_attn_sample",
    )(page_table, q_rows, kn_pad, cache_latent, ckr_t)


def _prepare_weights(w_in, w_uq, w_uk, w_uv, w_proj_a, w_proj_b, w_out, w_up, w_down):
    o = 3 * H_A * DH_A
    segs = [w_in[:, :o + Q_RANK + KV_RANK], w_in[:, o + Q_RANK + KV_RANK + ROPE_B:],
            w_in[:, o + Q_RANK + KV_RANK:o + Q_RANK + KV_RANK + ROPE_B],
            jnp.zeros((D_MODEL, LANES - ROPE_B), w_in.dtype)]
    w_in_p = jnp.concatenate(segs, axis=1).astype(BF16)
    wq = w_uq.reshape(Q_RANK, H_B, NOPE_B + ROPE_B)
    wuq_n = wq[:, :, :NOPE_B].reshape(Q_RANK, H_B * NOPE_B).astype(BF16)
    wuq_r = jnp.pad(wq[:, :, NOPE_B:], ((0, 0), (0, 0), (0, LANES - ROPE_B)))
    wuq_r = wuq_r.reshape(Q_RANK, H_B * LANES).astype(BF16)
    eye = jnp.eye(H_B, dtype=w_uk.dtype)
    wuk_bd = jnp.einsum('rhn,hg->hngr', w_uk, eye).reshape(H_B * NOPE_B, H_B * KV_RANK).astype(BF16)
    wuv_bd = jnp.einsum('rhv,hg->hrgv', w_uv, eye).reshape(H_B * KV_RANK, H_B * V_B).astype(BF16)
    return dict(w_in_p=w_in_p, wuq_n=wuq_n, wuq_r=wuq_r, wuk_bd=wuk_bd, wuv_bd=wuv_bd,
                wpa=w_proj_a.astype(BF16), wpb=w_proj_b.astype(BF16), wout=w_out.astype(BF16),
                wup=w_up.astype(BF16), wdn=w_down.astype(BF16))


def _row(g):
    return g.reshape(1, -1)


def _trunk(x2, mods, pos, attend, prompt, wts, g_pre_mix, g_post_mix, g_pre_mlp, g_post_mlp,
           g_q_lat, g_kv_lat):
    sh1, sc1, gt1, sh2, sc2, gt2 = mods
    tabs_a = _rope_tables(pos, ROT_A, THETA_A, DH_A)
    tabs_b = _rope_tables(pos, ROPE_B, THETA_B, ROPE_B)
    outs = _proj_in(x2, sc1, sh1, _row(g_pre_mix), wts['w_in_p'], _row(g_q_lat), _row(g_kv_lat),
                    wts['wuq_n'], wts['wuq_r'], wts['wuk_bd'], tabs_a, tabs_b, prompt)
    qa, k, v, latent, krope, ga, gb, qcat, kcat = outs[:9]
    oa, ob = attend(qa, k, v, qcat, kcat, outs[9:])
    x1 = _merge(oa, ob, ga, gb, x2, gt1, _row(g_post_mix), wts['wpa'], wts['wuv_bd'],
                wts['wpb'], wts['wout'])
    y = _mlp(x1, sc2, sh2, gt2, _row(g_pre_mlp), _row(g_post_mlp), wts['wup'], wts['wdn'])
    return y, k, v, latent, krope


def kernel(x_prompt, x_sample, c_prompt, c_sample, cache_k, cache_v, cache_latent, cache_krope, page_table, w_ada, b_ada, g_pre_mix, g_post_mix, g_pre_mlp, g_post_mlp, w_in, g_q_lat, g_kv_lat, w_uq, w_uk, w_uv, w_proj_a, w_proj_b, w_out, w_up, w_down):
    bsz, seq, _ = x_prompt.shape
    db, n_tok, _ = x_sample.shape
    n_pages = page_table.shape[1]
    past = n_pages * PAGE_SIZE
    assert bsz == 1 and seq % ROW_TILE == 0 and seq // MOBA_BLOCK <= N_SEL
    assert (db * n_tok) % ROW_TILE == 0 and n_tok <= 8
    assert past % MOBA_BLOCK == 0 and MOBA_TOPK <= past // MOBA_BLOCK <= LANES

    wts = _prepare_weights(w_in, w_uq, w_uk, w_uv, w_proj_a, w_proj_b, w_out, w_up, w_down)
    gains = (g_pre_mix, g_post_mix, g_pre_mlp, g_post_mlp, g_q_lat, g_kv_lat)

    n_c = bsz + db
    c_all = jnp.pad(jnp.concatenate([c_prompt, c_sample], axis=0), ((0, -n_c % 8), (0, 0)))
    mod_all = _modulation(c_all, w_ada, b_ada)
    mods_p = tuple(jnp.split(mod_all[:bsz], 6, axis=1))
    mods_s = tuple(jnp.split(jnp.repeat(mod_all[bsz:n_c], n_tok, axis=0), 6, axis=1))

    def attend_prompt(qa, k, v, qcat, kcat, extra):
        kaug, vaug, kmean = extra
        nb = kmean.shape[0]
        kmean_ext = jnp.pad(kmean.reshape(nb, 512), ((N_SEL, LANES - N_SEL - nb), (0, 0)))
        qaug = _select_prompt(qa, kmean_ext)
        return _moba_prompt(qaug, kaug, vaug), _mla_prompt(qcat, kcat)

    pos_p = jnp.arange(seq, dtype=jnp.int32)
    y_p, k_p, v_p, lat_p, kr_p = _trunk(x_prompt.reshape(seq, D_MODEL), mods_p, pos_p, attend_prompt,
                                        True, wts, *gains)

    rows_pad = 8
    ck_t = cache_k.transpose(0, 2, 3, 1)
    cv_t = cache_v.transpose(0, 2, 3, 1)
    ckr_t = cache_krope.transpose(0, 2, 1)

    def attend_sample(qa, k, v, qcat, kcat, extra):
        pad_tok = lambda a: jnp.pad(a.reshape(db, n_tok, a.shape[-1]),
                                    ((0, 0), (0, rows_pad - n_tok), (0, 0)))
        q_pad, kn_pad, vn_pad = pad_tok(qa), pad_tok(k), pad_tok(v)
        sel = _select_sample(page_table, q_pad, ck_t)
        sel = sel[:, :n_tok, :H_A * MOBA_TOPK].reshape(db, n_tok * H_A * MOBA_TOPK)
        oa = _moba_sample(page_table, sel, q_pad, kn_pad, vn_pad, ck_t, cv_t, n_tok)
        oa = oa[:, :n_tok].reshape(db * n_tok, H_A // 2, LANES).transpose(1, 0, 2).astype(BF16)
        q_rows = qcat.reshape(H_B, db, n_tok, 2 * LANES).transpose(1, 2, 0, 3)
        q_rows = q_rows.reshape(db, n_tok * H_B, 2 * LANES)
        ob = _mla_sample(page_table, q_rows, pad_tok(kcat), cache_latent, ckr_t, n_tok)
        return oa, ob.reshape(db * n_tok, H_B * KV_RANK)

    pos_s = past + jnp.tile(jnp.arange(n_tok, dtype=jnp.int32), db)
    y_s, k_s, v_s, lat_s, kr_s = _trunk(x_sample.reshape(db * n_tok, D_MODEL), mods_s, pos_s,
                                        attend_sample, False, wts, *gains)

    return (y_p.reshape(bsz, seq, D_MODEL), y_s.reshape(db, n_tok, D_MODEL),
            k_p.reshape(bsz, seq, H_A, DH_A), v_p.reshape(bsz, seq, H_A, DH_A),
            lat_p.reshape(bsz, seq, KV_RANK), kr_p.reshape(bsz, seq, ROPE_B),
            k_s.reshape(db, n_tok, H_A, DH_A), v_s.reshape(db, n_tok, H_A, DH_A),
            lat_s.reshape(db, n_tok, KV_RANK), kr_s.reshape(db, n_tok, ROPE_B))
```
